```python
import math
import jax, jax.numpy as jnp
from jax import lax
import numpy as np

D_MODEL = 1024
BATCH = 4
SEQ = 8192
DEPTH = 2

GRID_W = 64
CTX_LEN = 256
EPS = 1e-6
N_MOD = 9
D_FF = 2816
D_CONV = 512
CONV_W = 3
MLA_HEADS = 8
MLA_NOPE = 64
MLA_ROPE = 32
MLA_V = 64
MLA_Q_RANK = 256
MLA_KV_RANK = 128
MLA_SCALE = (MLA_NOPE + MLA_ROPE) ** -0.5
ROPE_BASE = 10000.0
Q_BLOCK = 128
EVEN_Q_START = 3 * D_CONV
EVEN_KV_START = EVEN_Q_START + MLA_Q_RANK
D_EVEN_IN = EVEN_KV_START + MLA_KV_RANK + MLA_ROPE
D_EVEN_OUT = D_CONV + MLA_HEADS * MLA_V
HGRN_HEADS = 8
HGRN_EXPAND = 128
D_HGRN = HGRN_HEADS * HGRN_EXPAND
HGRN_SCALE = HGRN_EXPAND ** -0.5
HGRN_CHUNK = 64
D_ODD_IN = 5 * D_HGRN
N_EVEN = (DEPTH + 1) // 2
N_ODD = DEPTH // 2

kernel_name = "hybrid_conv_mla_hgrn2_macaron_dit"


def rmsnorm(x, g):
    xf = x.astype(jnp.float32)
    y = xf * lax.rsqrt(jnp.mean(xf * xf, axis=-1, keepdims=True) + EPS)
    return (y * g.astype(jnp.float32)).astype(x.dtype)


def modulate(h, shift, scale):
    return h * (1 + scale) + shift


def swiglu(u, w1, w3, w2):
    return (jax.nn.silu(u @ w1) * (u @ w3)) @ w2


def ada_mod(cond, w, b):
    m = jax.nn.silu(cond) @ w + b
    return m.reshape(m.shape[:-1] + (N_MOD, D_MODEL))


def ffn_half(h, m, j, g, w1, w3, w2):
    u = modulate(rmsnorm(h, g), m[:, :, 3 * j], m[:, :, 3 * j + 1])
    return h + 0.5 * m[:, :, 3 * j + 2] * swiglu(u, w1, w3, w2)


def axial_rope_tables(rows):
    row = jnp.repeat(jnp.arange(rows, dtype=jnp.int32), GRID_W).astype(jnp.float32)
    col = jnp.tile(jnp.arange(GRID_W, dtype=jnp.int32), rows).astype(jnp.float32)
    n_freq = MLA_ROPE // 4
    inv = ROPE_BASE ** (-jnp.arange(n_freq, dtype=jnp.float32) / n_freq)
    ang = jnp.stack([row[:, None] * inv, col[:, None] * inv], axis=1)
    return jnp.cos(ang), jnp.sin(ang)


def apply_axial_rope(x, cos, sin):
    shp = x.shape
    xr = x.reshape(shp[:-1] + (2, 2, MLA_ROPE // 4))
    x1, x2 = xr[..., 0, :], xr[..., 1, :]
    c = cos[:, None].astype(x.dtype)
    s = sin[:, None].astype(x.dtype)
    return jnp.stack([x1 * c - x2 * s, x1 * s + x2 * c], axis=-2).reshape(shp)


def short_conv_centred(u, w):
    n = u.shape[1]
    up = jnp.pad(u, ((0, 0), (1, 1), (0, 0)))
    return up[:, :n] * w[0] + up[:, 1:n + 1] * w[1] + up[:, 2:] * w[2]


def short_conv_mixer(p, conv_w):
    gate_b, gate_c, v = jnp.split(p, 3, axis=-1)
    return gate_b * short_conv_centred(gate_c * v, conv_w)


def mla_queries(c_q, q_norm_g, w_uq):
    bsz, n = c_q.shape[:2]
    q = (rmsnorm(c_q, q_norm_g) @ w_uq).reshape(bsz, n, MLA_HEADS, MLA_NOPE + MLA_ROPE)
    return q[..., :MLA_NOPE], q[..., MLA_NOPE:]


def mla_keys_values(p_kv, kv_norm_g, w_ukv):
    bsz, n = p_kv.shape[:2]
    c_kv, k_r = p_kv[..., :MLA_KV_RANK], p_kv[..., MLA_KV_RANK:]
    kv = (rmsnorm(c_kv, kv_norm_g) @ w_ukv).reshape(bsz, n, MLA_HEADS, MLA_NOPE + MLA_V)
    return kv[..., :MLA_NOPE], k_r[:, :, None, :], kv[..., MLA_NOPE:]


def mla_attend(q_n, q_r, k_n, k_r, v):
    s = jnp.einsum('bqhd,bkhd->bhqk', q_n, k_n) + jnp.einsum('bqhd,bkd->bhqk', q_r, k_r)
    p = jax.nn.softmax(s.astype(jnp.float32) * MLA_SCALE, axis=-1).astype(v.dtype)
    return jnp.einsum('bhqk,bkhd->bqhd', p, v)


def mla_blocked(q_n, q_r, k_n, k_r, v):
    bsz, n, H, _ = q_n.shape
    nb = n // Q_BLOCK

    def to_blocks(a):
        return jnp.moveaxis(a.reshape(bsz, nb, Q_BLOCK, H, a.shape[-1]), 1, 0)

    o = lax.map(lambda qs: mla_attend(qs[0], qs[1], k_n, k_r, v), (to_blocks(q_n), to_blocks(q_r)))
    return jnp.moveaxis(o, 0, 1).reshape(bsz, n, H * v.shape[-1])


def even_mixer(h_x, h_c, w_in, conv_w, q_norm_g, w_uq, kv_norm_g, w_ukv, w_out, cos, sin, need_ctx_out):
    bsz, n_ctx = h_c.shape[:2]
    p_x = h_x @ w_in
    a_x = short_conv_mixer(p_x[..., :EVEN_Q_START], conv_w)
    qn_x, qr_x = mla_queries(p_x[..., EVEN_Q_START:EVEN_KV_START], q_norm_g, w_uq)
    kn_x, kr_x, v_x = mla_keys_values(p_x[..., EVEN_KV_START:], kv_norm_g, w_ukv)
    qr_x = apply_axial_rope(qr_x, cos, sin)
    kr_x = apply_axial_rope(kr_x, cos, sin)
    kn_c, kr_c, v_c = mla_keys_values(h_c @ w_in[:, EVEN_KV_START:], kv_norm_g, w_ukv)
    k_n = jnp.concatenate([kn_c, kn_x], axis=1)
    k_r = jnp.concatenate([kr_c, kr_x], axis=1)[:, :, 0]
    v = jnp.concatenate([v_c, v_x], axis=1)
    b_x = mla_blocked(qn_x, qr_x, k_n, k_r, v)
    y_x = jnp.concatenate([a_x, b_x], axis=-1) @ w_out
    if not need_ctx_out:
        return y_x, None
    p_c = h_c @ w_in[:, :EVEN_KV_START]
    a_c = short_conv_mixer(p_c[..., :EVEN_Q_START], conv_w)
    qn_c, qr_c = mla_queries(p_c[..., EVEN_Q_START:], q_norm_g, w_uq)
    b_c = mla_attend(qn_c, qr_c, kn_c, kr_c[:, :, 0], v_c).reshape(bsz, n_ctx, MLA_HEADS * MLA_V)
    y_c = jnp.concatenate([a_c, b_c], axis=-1) @ w_out
    return y_x, y_c


def gla_chunked(q, k, v, log_f, s0):
    bsz, n, H, _ = q.shape
    nc = n // HGRN_CHUNK

    def chunks(a):
        return a.reshape(bsz, nc, HGRN_CHUNK, H, a.shape[-1])

    q, k, v, log_f = chunks(q), chunks(k), chunks(v), chunks(log_f)
    b = jnp.cumsum(log_f, axis=2)
    b_last = b[:, :, -1:]
    q_t = q * jnp.exp(b)
    k_t = k * jnp.exp(-b)
    k_end = k * jnp.exp(b_last - b)
    mask = jnp.tril(jnp.ones((HGRN_CHUNK, HGRN_CHUNK), dtype=bool))
    att = jnp.where(mask, jnp.einsum('bnchk,bnshk->bnhcs', q_t, k_t), 0.0)
    o_intra = jnp.einsum('bnhcs,bnshv->bnchv', att, v)
    ds = jnp.einsum('bnchk,bnchv->nbhkv', k_end, v)
    decay = jnp.moveaxis(jnp.exp(b_last[:, :, 0]), 1, 0)

    def step(s, inp):
        a, d = inp
        return a[..., None] * s + d, s

    s_final, s_prev = lax.scan(step, s0, (decay, ds))
    o_inter = jnp.einsum('bnchk,nbhkv->bnchv', q_t, s_prev)
    return (o_intra + o_inter).reshape(bsz, n, H, v.shape[-1]), s_final


def context_final_state(k, v, log_f):
    b = jnp.cumsum(log_f, axis=1)
    return jnp.einsum('bthk,bthv->bhkv', k * jnp.exp(b[:, -1:] - b), v)


def odd_mixer(h_x, h_c, w_in, lb, g_norm_g, w_out, need_ctx_out):
    f32 = jnp.float32

    def heads(a):
        return a.reshape(a.shape[:2] + (HGRN_HEADS, HGRN_EXPAND)).astype(f32)

    lb = lb.reshape(2, HGRN_HEADS, HGRN_EXPAND)

    def gates(z, lb_dir):
        f = lb_dir + (1 - lb_dir) * jax.nn.sigmoid(z)
        return jnp.log(f), 1 - f

    def flip(a):
        return jnp.flip(a, axis=1)

    def readout(o, g, dtype):
        y = rmsnorm(o, g_norm_g).astype(dtype) * jax.nn.silu(g.reshape(o.shape).astype(dtype))
        return y.reshape(y.shape[:2] + (D_HGRN,)) @ w_out

    q_x, i_x, zf_x, zb_x, g_x = jnp.split(h_x @ w_in, 5, axis=-1)
    q_x, i_x = heads(q_x) * HGRN_SCALE, heads(i_x)
    lf_xf, k_xf = gates(heads(zf_x), lb[0])
    lf_xb, k_xb = gates(heads(zb_x), lb[1])
    i_c, zf_c, zb_c = jnp.split(h_c @ w_in[:, D_HGRN:4 * D_HGRN], 3, axis=-1)
    i_c = heads(i_c)
    lf_cf, k_cf = gates(heads(zf_c), lb[0])
    lf_cb, k_cb = gates(heads(zb_c), lb[1])
    y_c = None
    if need_ctx_out:
        q_c = heads(h_c @ w_in[:, :D_HGRN]) * HGRN_SCALE
        g_c = h_c @ w_in[:, 4 * D_HGRN:]
        zeros = jnp.zeros((h_c.shape[0], HGRN_HEADS, HGRN_EXPAND, HGRN_EXPAND), f32)
        o_cf, s_cf = gla_chunked(q_c, k_cf, i_c, lf_cf, zeros)
        o_cb, s_cb = gla_chunked(flip(q_c), flip(k_cb), flip(i_c), flip(lf_cb), zeros)
        y_c = readout(o_cf + flip(o_cb), g_c, h_c.dtype)
    else:
        s_cf = context_final_state(k_cf, i_c, lf_cf)
        s_cb = context_final_state(flip(k_cb), flip(i_c), flip(lf_cb))
    o_xf, _ = gla_chunked(q_x, k_xf, i_x, lf_xf, s_cf)
    o_xb, _ = gla_chunked(flip(q_x), flip(k_xb), flip(i_x), flip(lf_xb), s_cb)
    y_x = readout(o_xf + flip(o_xb), g_x, h_x.dtype)
    return y_x, y_c


def setup_inputs(seed: int = 0) -> dict:
    key = jax.random.key(seed)
    ks = jax.random.split(key, 22)

    def nrm(k, shape, scale=1.0):
        return jax.random.normal(k, shape, jnp.float32) * scale

    def w(k, shape, fan_in, scale=1.0):
        return nrm(k, shape, scale * fan_in ** -0.5)

    def gain(k, shape):
        return 1.0 + nrm(k, shape, 0.02)

    return {
        "x": nrm(ks[0], (BATCH, SEQ, D_MODEL)),
        "c": nrm(ks[1], (BATCH, D_MODEL)),
        "ctx": nrm(ks[2], (BATCH, CTX_LEN, D_MODEL)),
        "c_ctx": nrm(ks[3], (D_MODEL,)),
        "ada_w": w(ks[4], (DEPTH, D_MODEL, N_MOD * D_MODEL), D_MODEL, 0.5),
        "ada_b": nrm(ks[5], (DEPTH, N_MOD * D_MODEL), 0.02),
        "norm_g": gain(ks[6], (DEPTH, 3, D_MODEL)),
        "ffn_w1": w(ks[7], (DEPTH, 2, D_MODEL, D_FF), D_MODEL),
        "ffn_w3": w(ks[8], (DEPTH, 2, D_MODEL, D_FF), D_MODEL),
        "ffn_w2": w(ks[9], (DEPTH, 2, D_FF, D_MODEL), D_FF),
        "even_w_in": w(ks[10], (N_EVEN, D_MODEL, D_EVEN_IN), D_MODEL),
        "even_conv_w": w(ks[11], (N_EVEN, CONV_W, D_CONV), CONV_W),
        "mla_q_norm_g": gain(ks[12], (N_EVEN, MLA_Q_RANK)),
        "mla_w_uq": w(ks[13], (N_EVEN, MLA_Q_RANK, MLA_HEADS * (MLA_NOPE + MLA_ROPE)), MLA_Q_RANK),
        "mla_kv_norm_g": gain(ks[14], (N_EVEN, MLA_KV_RANK)),
        "mla_w_ukv": w(ks[15], (N_EVEN, MLA_KV_RANK, MLA_HEADS * (MLA_NOPE + MLA_V)), MLA_KV_RANK),
        "even_w_out": w(ks[16], (N_EVEN, D_EVEN_OUT, D_MODEL), D_EVEN_OUT),
        "odd_w_in": w(ks[17], (N_ODD, D_MODEL, D_ODD_IN), D_MODEL),
        "hgrn_lb_logits": nrm(ks[18], (DEPTH, 2, D_HGRN), 0.1),
        "hgrn_g_norm_g": gain(ks[19], (N_ODD, HGRN_EXPAND)),
        "odd_w_out": w(ks[20], (N_ODD, D_HGRN, D_MODEL), D_HGRN),
        "final_norm_g": gain(ks[21], (D_MODEL,)),
    }


def reference(x, c, ctx, c_ctx, ada_w, ada_b, norm_g, ffn_w1, ffn_w3, ffn_w2, even_w_in, even_conv_w,
              mla_q_norm_g, mla_w_uq, mla_kv_norm_g, mla_w_ukv, even_w_out, odd_w_in, hgrn_lb_logits,
              hgrn_g_norm_g, odd_w_out, final_norm_g):
    rows = x.shape[1] // GRID_W
    cos, sin = axial_rope_tables(rows)
    lb_p = jax.nn.softmax(hgrn_lb_logits.astype(jnp.float32), axis=0)
    lb_table = jnp.cumsum(lb_p, axis=0) - lb_p[0]
    h = ctx
    for l in range(DEPTH):
        need_ctx_out = l < DEPTH - 1
        mx = ada_mod(c, ada_w[l], ada_b[l])[:, None]
        mc = ada_mod(c_ctx, ada_w[l], ada_b[l])[None, None]
        x = ffn_half(x, mx, 0, norm_g[l, 0], ffn_w1[l, 0], ffn_w3[l, 0], ffn_w2[l, 0])
        h = ffn_half(h, mc, 0, norm_g[l, 0], ffn_w1[l, 0], ffn_w3[l, 0], ffn_w2[l, 0])
        ux = modulate(rmsnorm(x, norm_g[l, 1]), mx[:, :, 3], mx[:, :, 4])
        uh = modulate(rmsnorm(h, norm_g[l, 1]), mc[:, :, 3], mc[:, :, 4])
        if l % 2 == 0:
            e = l // 2
            y_x, y_c = even_mixer(ux, uh, even_w_in[e], even_conv_w[e], mla_q_norm_g[e], mla_w_uq[e],
                                  mla_kv_norm_g[e], mla_w_ukv[e], even_w_out[e], cos, sin, need_ctx_out)
        else:
            o = l // 2
            y_x, y_c = odd_mixer(ux, uh, odd_w_in[o], lb_table[l], hgrn_g_norm_g[o], odd_w_out[o], need_ctx_out)
        x = x + mx[:, :, 5] * y_x
        x = ffn_half(x, mx, 2, norm_g[l, 2], ffn_w1[l, 1], ffn_w3[l, 1], ffn_w2[l, 1])
        if need_ctx_out:
            h = h + mc[:, :, 5] * y_c
            h = ffn_half(h, mc, 2, norm_g[l, 2], ffn_w1[l, 1], ffn_w3[l, 1], ffn_w2[l, 1])
    return rmsnorm(x, final_norm_g)
```

```python
import functools

import jax
import jax.numpy as jnp
from jax import lax
from jax.experimental import pallas as pl
from jax.experimental.pallas import tpu as pltpu

F32 = jnp.float32
BF16 = jnp.bfloat16

EPS = 1e-6
N_MOD = 9
GRID_W = 64
ROPE_BASE = 10000.0
LANES = 128

D_CONV = 512
MLA_HEADS = 8
MLA_NOPE = 64
MLA_ROPE = 32
MLA_V = 64
MLA_Q_RANK = 256
MLA_KV_RANK = 128
MLA_SCALE = (MLA_NOPE + MLA_ROPE) ** -0.5
Q_OFF = 3 * D_CONV
KV_OFF = Q_OFF + MLA_Q_RANK
KR_OFF = KV_OFF + MLA_KV_RANK
D_EVEN_PROJ = KR_OFF + LANES
HALO = 16

HGRN_HEADS = 8
HGRN_EXPAND = 128
D_HGRN = HGRN_HEADS * HGRN_EXPAND
HGRN_SCALE = HGRN_EXPAND ** -0.5
HGRN_CHUNK = 64

VMEM_LIMIT = 56 * 1024 * 1024


def _dot(a, b):
    return jnp.dot(a, b, preferred_element_type=F32)


def _dot_nt(a, b):
    return lax.dot_general(a, b, (((1,), (1,)), ((), ())), preferred_element_type=F32)


def _dot_tn(a, b):
    return lax.dot_general(a, b, (((0,), (0,)), ((), ())), preferred_element_type=F32)


def _silu(v):
    return v * jax.nn.sigmoid(v)


def _rms(v):
    return v * lax.rsqrt(jnp.mean(v * v, axis=-1, keepdims=True) + EPS)


def _norm_mod(x, g, shift, scale):
    return _rms(x) * (g * (1.0 + scale)) + shift


def _const_spec(shape):
    nd = len(shape)
    return pl.BlockSpec(shape, lambda *_: (0,) * nd, pipeline_mode=pl.Buffered(1))


def _params(*sem):
    return pltpu.CompilerParams(dimension_semantics=sem, vmem_limit_bytes=VMEM_LIMIT)


def _ada_kernel(cond_ref, w_ref, b_ref, o_ref):
    s = _silu(cond_ref[...]).astype(BF16)
    o_ref[0] = _dot(s, w_ref[0].astype(BF16)) + b_ref[0]


def _ada_mods(cond, ada_w, ada_b):
    depth, d, n = ada_w.shape
    rows = cond.shape[0]
    tn = 1024
    return pl.pallas_call(
        _ada_kernel,
        grid=(depth, n // tn),
        in_specs=[
            pl.BlockSpec((rows, d), lambda l, j: (0, 0)),
            pl.BlockSpec((1, d, tn), lambda l, j: (l, 0, j)),
            pl.BlockSpec((1, 1, tn), lambda l, j: (l, 0, j)),
        ],
        out_specs=pl.BlockSpec((1, rows, tn), lambda l, j: (l, 0, j)),
        out_shape=jax.ShapeDtypeStruct((depth, rows, n), F32),
        compiler_params=_params("arbitrary", "arbitrary"),
        name="ada_mods",
    )(cond, ada_w, ada_b.reshape(depth, 1, n))


def _ffn_kernel(*refs, j, tf, mix, final):
    refs = list(refs)
    x_ref = refs.pop(0)
    if mix:
        a_ref, b_ref, wo_ref = refs.pop(0), refs.pop(0), refs.pop(0)
    mods_ref, g_ref, w1_ref, w3_ref, w2_ref = refs[:5]
    refs = refs[5:]
    if final:
        gf_ref = refs.pop(0)
    o_ref = refs.pop(0)

    x = x_ref[0]
    ms = mods_ref[0]
    if mix:
        dc = a_ref.shape[-1]
        y = _dot(a_ref[0], wo_ref[:dc, :]) + _dot(b_ref[0], wo_ref[dc:, :])
        x = x + ms[5:6] * y
    u = _norm_mod(x, g_ref[...], ms[3 * j:3 * j + 1], ms[3 * j + 1:3 * j + 2]).astype(BF16)
    acc = jnp.zeros(x.shape, F32)
    for c in range(w1_ref.shape[1] // tf):
        sl = slice(c * tf, (c + 1) * tf)
        h = (_silu(_dot(u, w1_ref[:, sl])) * _dot(u, w3_ref[:, sl])).astype(BF16)
        acc = acc + _dot(h, w2_ref[sl, :])
    out = x + (0.5 * ms[3 * j + 2:3 * j + 3]) * acc
    if final:
        out = _rms(out) * gf_ref[...]
    o_ref[0] = out


def _ffn_half(x, mods, g, w1, w3, w2, *, j, mix=None, final_g=None, tm=512):
    bsz, n, d = x.shape
    f = w1.shape[1]
    tm = min(tm, n)
    per_batch = mods.shape[0] > 1
    mods_map = (lambda b, t: (b, 0, 0)) if per_batch else (lambda b, t: (0, 0, 0))
    row = lambda b, t: (b, t, 0)
    args = [x]
    specs = [pl.BlockSpec((1, tm, d), row)]
    if mix is not None:
        a, bb, wo = mix
        args += [a, bb, wo]
        specs += [pl.BlockSpec((1, tm, a.shape[-1]), row), pl.BlockSpec((1, tm, bb.shape[-1]), row),
                  _const_spec(wo.shape)]
    args += [mods, g.reshape(1, d), w1, w3, w2]
    specs += [pl.BlockSpec((1, N_MOD, d), mods_map), _const_spec((1, d)), _const_spec(w1.shape),
              _const_spec(w3.shape), _const_spec(w2.shape)]
    if final_g is not None:
        args.append(final_g.reshape(1, d))
        specs.append(_const_spec((1, d)))
    kern = functools.partial(_ffn_kernel, j=j, tf=256, mix=mix is not None, final=final_g is not None)
    return pl.pallas_call(
        kern,
        grid=(bsz, n // tm),
        in_specs=specs,
        out_specs=pl.BlockSpec((1, tm, d), row),
        out_shape=jax.ShapeDtypeStruct((bsz, n, d), F32),
        compiler_params=_params("parallel", "parallel"),
        name="ffn_half",
    )(*args)


def _even_proj_kernel(x_ref, xp_ref, xn_ref, mods_ref, g_ref, win_ref, cw_ref, gq_ref, wq_ref, gkv_ref, wkv_ref,
                      cos_ref, sin_ref, a_ref, q_ref, k_ref, v_ref, *, n_tok):
    tm = x_ref.shape[1]
    t = pl.program_id(1)
    ms = mods_ref[0]
    xe = jnp.concatenate([xp_ref[0], x_ref[0], xn_ref[0]], axis=0)
    u = _norm_mod(xe, g_ref[...], ms[3:4], ms[4:5]).astype(BF16)
    p = _dot(u, win_ref[...])

    gcv = p[:, D_CONV:2 * D_CONV] * p[:, 2 * D_CONV:3 * D_CONV]
    rows = lax.broadcasted_iota(jnp.int32, (tm + 2 * HALO, 1), 0) + (t * tm - HALO)
    gcv = jnp.where((rows >= 0) & (rows < n_tok), gcv, 0.0)
    prev = pltpu.roll(gcv, 1, 0)[HALO:HALO + tm]
    nxt = pltpu.roll(gcv, tm + 2 * HALO - 1, 0)[HALO:HALO + tm]
    cw = cw_ref[...]
    conv = prev * cw[0:1] + gcv[HALO:HALO + tm] * cw[1:2] + nxt * cw[2:3]
    pm = p[HALO:HALO + tm]
    a_ref[0] = (pm[:, :D_CONV] * conv).astype(BF16)

    cos = cos_ref[...]
    sin = sin_ref[...]
    qa = _dot((_rms(pm[:, Q_OFF:KV_OFF]) * (gq_ref[...] * MLA_SCALE)).astype(BF16), wq_ref[...])
    kv = _dot((_rms(pm[:, KV_OFF:KR_OFF]) * gkv_ref[...]).astype(BF16), wkv_ref[...])
    krb = pm[:, KR_OFF:]
    kr = krb * cos + pltpu.roll(krb, LANES - MLA_ROPE, 1) * sin
    lane = lax.broadcasted_iota(jnp.int32, (tm, LANES), 1)
    low = lane < MLA_NOPE
    for h in range(MLA_HEADS):
        sl = slice(h * LANES, (h + 1) * LANES)
        qh = qa[:, sl]
        q_ref[0, h] = (qh * cos + pltpu.roll(qh, LANES - MLA_ROPE, 1) * sin).astype(BF16)
        kvh = kv[:, sl]
        k_ref[0, h] = jnp.where(low, kvh, kr).astype(BF16)
        if h % 2 == 0:
            vh = jnp.where(low, pltpu.roll(kvh, MLA_NOPE, 1), jnp.where(lane == MLA_V, 1.0, 0.0))
        else:
            vh = jnp.where(low, jnp.where(lane == 0, 1.0, 0.0), kvh)
        v_ref[0, h] = vh.astype(BF16)


def _even_proj(x, mods, g, w_in, conv_w, gq, wq, gkv, wkv, cos, sin, *, tm=512):
    bsz, n, d = x.shape
    tm = min(tm, n)
    nh = tm // HALO
    last = n // HALO - 1
    per_batch = mods.shape[0] > 1
    mods_map = (lambda b, t: (b, 0, 0)) if per_batch else (lambda b, t: (0, 0, 0))
    row = lambda b, t: (b, t, 0)
    head_spec = pl.BlockSpec((1, MLA_HEADS, tm, LANES), lambda b, t: (b, 0, t, 0))
    head_shape = jax.ShapeDtypeStruct((bsz, MLA_HEADS, n, LANES), BF16)
    return pl.pallas_call(
        functools.partial(_even_proj_kernel, n_tok=n),
        grid=(bsz, n // tm),
        in_specs=[
            pl.BlockSpec((1, tm, d), row),
            pl.BlockSpec((1, HALO, d), lambda b, t: (b, jnp.maximum(t * nh - 1, 0), 0)),
            pl.BlockSpec((1, HALO, d), lambda b, t: (b, jnp.minimum((t + 1) * nh, last), 0)),
            pl.BlockSpec((1, N_MOD, d), mods_map),
            _const_spec((1, d)),
            _const_spec(w_in.shape),
            _const_spec(conv_w.shape),
            _const_spec((1, MLA_Q_RANK)),
            _const_spec(wq.shape),
            _const_spec((1, MLA_KV_RANK)),
            _const_spec(wkv.shape),
            pl.BlockSpec((tm, LANES), lambda b, t: (t, 0)),
            pl.BlockSpec((tm, LANES), lambda b, t: (t, 0)),
        ],
        out_specs=[pl.BlockSpec((1, tm, D_CONV), row), head_spec, head_spec, head_spec],
        out_shape=[jax.ShapeDtypeStruct((bsz, n, D_CONV), BF16), head_shape, head_shape, head_shape],
        compiler_params=_params("parallel", "parallel"),
        name="even_proj",
    )(x, x, x, mods, g.reshape(1, d), w_in, conv_w, gq.reshape(1, -1), wq, gkv.reshape(1, -1), wkv, cos, sin)


def _attn_kernel(*refs, seg_chunks):
    q_ref = refs[0]
    nseg = len(seg_chunks)
    kv_refs = refs[1:1 + 2 * nseg]
    o_ref = refs[1 + 2 * nseg]
    m_ref, acc_ref = refs[2 + 2 * nseg:]
    tq = q_ref.shape[2]

    m_ref[...] = jnp.full(m_ref.shape, -jnp.inf, F32)
    acc_ref[...] = jnp.zeros(acc_ref.shape, F32)

    for s, (tk, nk) in enumerate(seg_chunks):
        k_ref, v_ref = kv_refs[2 * s], kv_refs[2 * s + 1]

        def step(c, carry, k_ref=k_ref, v_ref=v_ref, tk=tk):
            off = pl.multiple_of(c * tk, tk)
            for hh in range(2):
                k = k_ref[0, hh, pl.ds(off, tk), :]
                v = v_ref[0, hh, pl.ds(off, tk), :]
                sc = _dot_nt(q_ref[0, hh], k)
                m_prev = m_ref[hh]
                m_new = jnp.maximum(m_prev, jnp.max(sc, axis=1, keepdims=True))
                p = jnp.exp(sc - pltpu.repeat(m_new, tk // LANES, 1))
                acc_ref[hh] = acc_ref[hh] * jnp.exp(m_prev - m_new) + _dot(p.astype(BF16), v)
                m_ref[hh] = m_new
            return carry

        if nk == 1:
            step(0, 0)
        else:
            lax.fori_loop(0, nk, step, 0)

    lane = lax.broadcasted_iota(jnp.int32, (tq, LANES), 1)
    acc_e, acc_o = acc_ref[0], acc_ref[1]
    den_e = jnp.sum(jnp.where(lane == MLA_V, acc_e, 0.0), axis=1, keepdims=True)
    den_o = jnp.sum(jnp.where(lane == 0, acc_o, 0.0), axis=1, keepdims=True)
    o_ref[0] = jnp.where(lane < MLA_V, acc_e / den_e, acc_o / den_o).astype(o_ref.dtype)


def _attention(q, segs, *, tq=512, tk=512):
    bsz, nh, n, _ = q.shape
    tq = min(tq, n)
    args = [q]
    specs = [pl.BlockSpec((1, 2, tq, LANES), lambda b, h, t: (b, h, t, 0))]
    seg_chunks = []
    for k, v in segs:
        ns = k.shape[2]
        tks = min(tk, ns)
        seg_chunks.append((tks, ns // tks))
        spec = pl.BlockSpec((1, 2, ns, LANES), lambda b, h, t: (b, h, 0, 0))
        args += [k, v]
        specs += [spec, spec]
    return pl.pallas_call(
        functools.partial(_attn_kernel, seg_chunks=tuple(seg_chunks)),
        grid=(bsz, nh // 2, n // tq),
        in_specs=specs,
        out_specs=pl.BlockSpec((1, tq, LANES), lambda b, h, t: (b, t, h)),
        out_shape=jax.ShapeDtypeStruct((bsz, n, nh // 2 * LANES), BF16),
        scratch_shapes=[pltpu.VMEM((2, tq, LANES), F32), pltpu.VMEM((2, tq, LANES), F32)],
        compiler_params=_params("parallel", "parallel", "parallel"),
        name="mla_attention",
    )(*args)


def _split3(v):
    hi = v.astype(BF16)
    r = v - hi.astype(F32)
    mid = r.astype(BF16)
    lo = (r - mid.astype(F32)).astype(BF16)
    return hi, mid, lo


def _gla_chunk(st_ref, tri, lb, z, val, q, reverse):
    f = lb + (1.0 - lb) * jax.nn.sigmoid(z)
    kk = 1.0 - f
    hi, mid, lo = _split3(jnp.log(f))
    b = _dot(tri, hi) + _dot(tri, mid) + _dot(tri, lo)
    btot = b[0:1] if reverse else b[HGRN_CHUNK - 1:HGRN_CHUNK]
    k_end = (kk * jnp.exp(btot - b)).astype(BF16)
    decay = jnp.exp(btot)
    vb = val.astype(BF16)
    outs = []
    if q is not None:
        q_t = (q * jnp.exp(b)).astype(BF16)
        k_t = (kk * jnp.exp(-b)).astype(BF16)
        r = lax.broadcasted_iota(jnp.int32, (HGRN_CHUNK, HGRN_CHUNK), 0)
        c = lax.broadcasted_iota(jnp.int32, (HGRN_CHUNK, HGRN_CHUNK), 1)
        mask = (c >= r) if reverse else (c <= r)
    for h in range(HGRN_HEADS):
        sl = slice(h * HGRN_EXPAND, (h + 1) * HGRN_EXPAND)
        st = st_ref[h]
        if q is not None:
            att = jnp.where(mask, _dot_nt(q_t[:, sl], k_t[:, sl]), 0.0).astype(BF16)
            outs.append(_dot(att, vb[:, sl]) + _dot_nt(q_t[:, sl], st.astype(BF16)))
        st_ref[h] = st * decay[:, sl] + _dot_tn(vb[:, sl], k_end[:, sl])
    return outs


def _hgrn_kernel(*refs, reverse, final):
    refs = list(refs)
    x_ref, xc_ref, mods_ref, modc_ref, g_ref, win_ref, lb_ref = refs[:7]
    refs = refs[7:]
    if final:
        ob_ref, gn_ref, wo_ref = refs[:3]
        refs = refs[3:]
    o_ref = refs.pop(0)
    st_ref = refs.pop(0)
    if final:
        of_ref = refs.pop(0)

    t = pl.program_id(1)
    tm = x_ref.shape[1]
    lb = lb_ref[...]
    r = lax.broadcasted_iota(jnp.int32, (HGRN_CHUNK, HGRN_CHUNK), 0)
    c = lax.broadcasted_iota(jnp.int32, (HGRN_CHUNK, HGRN_CHUNK), 1)
    tri = jnp.where((c >= r) if reverse else (c <= r), 1.0, 0.0).astype(BF16)

    def chunk_order(n):
        idx = range(n // HGRN_CHUNK)
        return reversed(idx) if reverse else idx

    @pl.when(t == 0)
    def _():
        st_ref[...] = jnp.zeros(st_ref.shape, F32)
        mc = modc_ref[0]
        u = _norm_mod(xc_ref[0], g_ref[...], mc[3:4], mc[4:5]).astype(BF16)
        p = _dot(u, win_ref[:, D_HGRN:3 * D_HGRN])
        for ci in chunk_order(xc_ref.shape[1]):
            rs = slice(ci * HGRN_CHUNK, (ci + 1) * HGRN_CHUNK)
            _gla_chunk(st_ref, tri, lb, p[rs, D_HGRN:], p[rs, :D_HGRN], None, reverse)

    @pl.when(t > 0)
    def _():
        ms = mods_ref[0]
        x = x_ref[0]
        u = _norm_mod(x, g_ref[...], ms[3:4], ms[4:5]).astype(BF16)
        p = _dot(u, win_ref[...])
        dst = of_ref if final else o_ref.at[0]
        for ci in chunk_order(tm):
            rs = slice(ci * HGRN_CHUNK, (ci + 1) * HGRN_CHUNK)
            outs = _gla_chunk(st_ref, tri, lb, p[rs, 2 * D_HGRN:3 * D_HGRN], p[rs, D_HGRN:2 * D_HGRN],
                              p[rs, :D_HGRN] * HGRN_SCALE, reverse)
            for h, oh in enumerate(outs):
                dst[rs, h * HGRN_EXPAND:(h + 1) * HGRN_EXPAND] = oh
        if final:
            o = of_ref[...] + ob_ref[0]
            gn = gn_ref[...]
            ys = [_rms(o[:, h * HGRN_EXPAND:(h + 1) * HGRN_EXPAND]) * gn for h in range(HGRN_HEADS)]
            y = (jnp.concatenate(ys, axis=1) * _silu(p[:, 3 * D_HGRN:])).astype(BF16)
            o_ref[0] = x + ms[5:6] * _dot(y, wo_ref[...])


def _hgrn_pass(x, xc, mods_x, mods_c, g, w_in, lb, *, reverse, final=None, tm=256):
    bsz, n, d = x.shape
    nc = xc.shape[1]
    tm = min(tm, n)
    nt = n // tm
    if reverse:
        xmap = lambda b, t: (b, nt - 1 - jnp.maximum(t - 1, 0), 0)
    else:
        xmap = lambda b, t: (b, jnp.maximum(t - 1, 0), 0)
    args = [x, xc, mods_x, mods_c, g.reshape(1, d), w_in, lb.reshape(1, D_HGRN)]
    specs = [
        pl.BlockSpec((1, tm, d), xmap),
        pl.BlockSpec((1, nc, d), lambda b, t: (b, 0, 0)),
        pl.BlockSpec((1, N_MOD, d), lambda b, t: (b, 0, 0)),
        pl.BlockSpec((1, N_MOD, d), lambda b, t: (0, 0, 0)),
        _const_spec((1, d)),
        _const_spec(w_in.shape),
        _const_spec((1, D_HGRN)),
    ]
    scratch = [pltpu.VMEM((HGRN_HEADS, HGRN_EXPAND, HGRN_EXPAND), F32)]
    if final is not None:
        ob, gn, wo = final
        args += [ob, gn.reshape(1, HGRN_EXPAND), wo]
        specs += [pl.BlockSpec((1, tm, D_HGRN), xmap), _const_spec((1, HGRN_EXPAND)), _const_spec(wo.shape)]
        scratch.append(pltpu.VMEM((tm, D_HGRN), F32))
        out_d = d
    else:
        out_d = D_HGRN
    return pl.pallas_call(
        functools.partial(_hgrn_kernel, reverse=reverse, final=final is not None),
        grid=(bsz, nt + 1),
        in_specs=specs,
        out_specs=pl.BlockSpec((1, tm, out_d), xmap),
        out_shape=jax.ShapeDtypeStruct((bsz, n, out_d), F32),
        scratch_shapes=scratch,
        compiler_params=_params("parallel", "arbitrary"),
        name="hgrn_bwd" if reverse else "hgrn_fwd",
    )(*args)


def _rope_tables(n):
    rows = n // GRID_W
    row = jnp.repeat(jnp.arange(rows, dtype=jnp.int32), GRID_W).astype(F32)
    col = jnp.tile(jnp.arange(GRID_W, dtype=jnp.int32), rows).astype(F32)
    n_freq = MLA_ROPE // 4
    inv = ROPE_BASE ** (-jnp.arange(n_freq, dtype=F32) / n_freq)
    ar, ac = row[:, None] * inv, col[:, None] * inv
    cos = jnp.concatenate([jnp.cos(ar), jnp.cos(ar), jnp.cos(ac), jnp.cos(ac)], axis=1)
    sin = jnp.concatenate([-jnp.sin(ar), jnp.sin(ar), -jnp.sin(ac), jnp.sin(ac)], axis=1)
    ones = jnp.ones((n, MLA_NOPE), F32)
    zeros = jnp.zeros((n, MLA_ROPE), F32)
    return (jnp.concatenate([ones, cos, zeros], axis=1),
            jnp.concatenate([jnp.zeros((n, MLA_NOPE), F32), sin, zeros], axis=1))


def _swap_rope_cols(w):
    q = MLA_ROPE // 4
    return jnp.concatenate([w[:, q:2 * q], w[:, :q], w[:, 3 * q:], w[:, 2 * q:3 * q]], axis=1)


def _even_weights(w_in, w_uq):
    kr = w_in[:, KR_OFF:]
    w_in = jnp.concatenate([w_in[:, :KR_OFF], jnp.zeros((w_in.shape[0], MLA_NOPE), w_in.dtype), kr,
                            _swap_rope_cols(kr)], axis=1)
    wq = w_uq.reshape(MLA_Q_RANK, MLA_HEADS, MLA_NOPE + MLA_ROPE)
    blocks = []
    for h in range(MLA_HEADS):
        rope = wq[:, h, MLA_NOPE:]
        blocks += [wq[:, h, :MLA_NOPE], rope, _swap_rope_cols(rope)]
    return w_in.astype(BF16), jnp.concatenate(blocks, axis=1).astype(BF16)


def kernel(x, c, ctx, c_ctx, ada_w, ada_b, norm_g, ffn_w1, ffn_w3, ffn_w2, even_w_in, even_conv_w, mla_q_norm_g,
           mla_w_uq, mla_kv_norm_g, mla_w_ukv, even_w_out, odd_w_in, hgrn_lb_logits, hgrn_g_norm_g, odd_w_out,
           final_norm_g):
    bsz, n, d = x.shape
    depth = ada_w.shape[0]
    n_ctx = ctx.shape[1]

    cond = jnp.concatenate([c, c_ctx[None], jnp.zeros((8 - bsz - 1, d), F32)], axis=0)
    mods = _ada_mods(cond, ada_w, ada_b).reshape(depth, 8, N_MOD, d)

    lb_p = jax.nn.softmax(hgrn_lb_logits.astype(F32), axis=0)
    lb_table = jnp.cumsum(lb_p, axis=0) - lb_p[0]

    w1, w3, w2 = ffn_w1.astype(BF16), ffn_w3.astype(BF16), ffn_w2.astype(BF16)
    cos_x, sin_x = _rope_tables(n)
    cos_c = jnp.concatenate([jnp.ones((n_ctx, MLA_NOPE + MLA_ROPE), F32), jnp.zeros((n_ctx, MLA_ROPE), F32)], axis=1)
    sin_c = jnp.zeros((n_ctx, LANES), F32)

    h = ctx
    for l in range(depth):
        need_ctx_out = l < depth - 1
        mx, mc = mods[l, :bsz], mods[l, bsz:bsz + 1]
        last = l == depth - 1
        x = _ffn_half(x, mx, norm_g[l, 0], w1[l, 0], w3[l, 0], w2[l, 0], j=0)
        h = _ffn_half(h, mc, norm_g[l, 0], w1[l, 0], w3[l, 0], w2[l, 0], j=0)
        if l % 2 == 0:
            e = l // 2
            w_in, wq = _even_weights(even_w_in[e], mla_w_uq[e])
            wkv, wo = mla_w_ukv[e].astype(BF16), even_w_out[e].astype(BF16)
            proj = functools.partial(_even_proj, g=norm_g[l, 1], w_in=w_in, conv_w=even_conv_w[e],
                                     gq=mla_q_norm_g[e], wq=wq, gkv=mla_kv_norm_g[e], wkv=wkv)
            a_x, q_x, k_x, v_x = proj(x, mx, cos=cos_x, sin=sin_x)
            a_c, q_c, k_c, v_c = proj(h, mc, cos=cos_c, sin=sin_c)
            b_x = _attention(q_x, [(k_c, v_c), (k_x, v_x)])
            x = _ffn_half(x, mx, norm_g[l, 2], w1[l, 1], w3[l, 1], w2[l, 1], j=2, mix=(a_x, b_x, wo),
                          final_g=final_norm_g if last else None)
            if need_ctx_out:
                b_c = _attention(q_c, [(k_c, v_c)])
                h = _ffn_half(h, mc, norm_g[l, 2], w1[l, 1], w3[l, 1], w2[l, 1], j=2, mix=(a_c, b_c, wo))
        else:
            if need_ctx_out:
                raise NotImplementedError("context output of an HGRN2 layer is only needed for depth > 2")
            o = l // 2
            w_in = odd_w_in[o].astype(BF16)
            lb = lb_table[l]
            w_bwd = jnp.concatenate([w_in[:, :2 * D_HGRN], w_in[:, 3 * D_HGRN:4 * D_HGRN]], axis=1)
            w_fwd = jnp.concatenate([w_in[:, :3 * D_HGRN], w_in[:, 4 * D_HGRN:]], axis=1)
            o_b = _hgrn_pass(x, h, mx, mc, norm_g[l, 1], w_bwd, lb[1], reverse=True)
            x = _hgrn_pass(x, h, mx, mc, norm_g[l, 1], w_fwd, lb[0], reverse=False,
                           final=(o_b, hgrn_g_norm_g[o], odd_w_out[o].astype(BF16)))
            x = _ffn_half(x, mx, norm_g[l, 2], w1[l, 1], w3[l, 1], w2[l, 1], j=2,
                          final_g=final_norm_g if last else None)
    return x
```

```python
import functools

import jax
import jax.numpy as jnp
from jax import lax
from jax.experimental import pallas as pl
from jax.experimental.pallas import tpu as pltpu

F32 = jnp.float32
BF16 = jnp.bfloat16

EPS = 1e-6
N_MOD = 9
GRID_W = 64
ROPE_BASE = 10000.0
LANES = 128

D_CONV = 512
MLA_HEADS = 8
MLA_NOPE = 64
MLA_ROPE = 32
MLA_V = 64
MLA_Q_RANK = 256
MLA_KV_RANK = 128
MLA_SCALE = (MLA_NOPE + MLA_ROPE) ** -0.5
Q_OFF = 3 * D_CONV
KV_OFF = Q_OFF + MLA_Q_RANK
KR_OFF = KV_OFF + MLA_KV_RANK
D_EVEN_PROJ = KR_OFF + LANES
HALO = 16

HGRN_HEADS = 8
HGRN_EXPAND = 128
D_HGRN = HGRN_HEADS * HGRN_EXPAND
HGRN_SCALE = HGRN_EXPAND ** -0.5
HGRN_CHUNK = 64

VMEM_LIMIT = 56 * 1024 * 1024


def _dot(a, b):
    return jnp.dot(a, b, preferred_element_type=F32)


def _dot_nt(a, b):
    return lax.dot_general(a, b, (((1,), (1,)), ((), ())), preferred_element_type=F32)


def _dot_tn(a, b):
    return lax.dot_general(a, b, (((0,), (0,)), ((), ())), preferred_element_type=F32)


def _silu(v):
    return v * jax.nn.sigmoid(v)


def _rms(v):
    return v * lax.rsqrt(jnp.mean(v * v, axis=-1, keepdims=True) + EPS)


def _norm_mod(x, g, shift, scale):
    return _rms(x) * (g * (1.0 + scale)) + shift


def _const_spec(shape):
    nd = len(shape)
    return pl.BlockSpec(shape, lambda *_: (0,) * nd, pipeline_mode=pl.Buffered(1))


def _params(*sem):
    return pltpu.CompilerParams(dimension_semantics=sem, vmem_limit_bytes=VMEM_LIMIT)


def _ada_kernel(cond_ref, w_ref, b_ref, o_ref):
    s = _silu(cond_ref[...]).astype(BF16)
    o_ref[0] = _dot(s, w_ref[0].astype(BF16)) + b_ref[0]


def _ada_mods(cond, ada_w, ada_b):
    depth, d, n = ada_w.shape
    rows = cond.shape[0]
    tn = 1024
    return pl.pallas_call(
        _ada_kernel,
        grid=(depth, n // tn),
        in_specs=[
            pl.BlockSpec((rows, d), lambda l, j: (0, 0)),
            pl.BlockSpec((1, d, tn), lambda l, j: (l, 0, j)),
            pl.BlockSpec((1, 1, tn), lambda l, j: (l, 0, j)),
        ],
        out_specs=pl.BlockSpec((1, rows, tn), lambda l, j: (l, 0, j)),
        out_shape=jax.ShapeDtypeStruct((depth, rows, n), F32),
        compiler_params=_params("arbitrary", "arbitrary"),
        name="ada_mods",
    )(cond, ada_w, ada_b.reshape(depth, 1, n))


def _ffn_kernel(*refs, j, tf, mix, final):
    refs = list(refs)
    x_ref = refs.pop(0)
    if mix:
        a_ref, b_ref, wo_ref = refs.pop(0), refs.pop(0), refs.pop(0)
    mods_ref, g_ref, w1_ref, w3_ref, w2_ref = refs[:5]
    refs = refs[5:]
    if final:
        gf_ref = refs.pop(0)
    o_ref = refs.pop(0)

    x = x_ref[0]
    ms = mods_ref[0]
    if mix:
        dc = a_ref.shape[-1]
        y = _dot(a_ref[0], wo_ref[:dc, :]) + _dot(b_ref[0], wo_ref[dc:, :])
        x = x + ms[5:6] * y
    u = _norm_mod(x, g_ref[...], ms[3 * j:3 * j + 1], ms[3 * j + 1:3 * j + 2]).astype(BF16)
    acc = jnp.zeros(x.shape, F32)
    for c in range(w1_ref.shape[1] // tf):
        sl = slice(c * tf, (c + 1) * tf)
        h = (_silu(_dot(u, w1_ref[:, sl])) * _dot(u, w3_ref[:, sl])).astype(BF16)
        acc = acc + _dot(h, w2_ref[sl, :])
    out = x + (0.5 * ms[3 * j + 2:3 * j + 3]) * acc
    if final:
        out = _rms(out) * gf_ref[...]
    o_ref[0] = out


def _ffn_half(x, mods, g, w1, w3, w2, *, j, mix=None, final_g=None, tm=512):
    bsz, n, d = x.shape
    f = w1.shape[1]
    tm = min(tm, n)
    per_batch = mods.shape[0] > 1
    mods_map = (lambda b, t: (b, 0, 0)) if per_batch else (lambda b, t: (0, 0, 0))
    row = lambda b, t: (b, t, 0)
    args = [x]
    specs = [pl.BlockSpec((1, tm, d), row)]
    if mix is not None:
        a, bb, wo = mix
        args += [a, bb, wo]
        specs += [pl.BlockSpec((1, tm, a.shape[-1]), row), pl.BlockSpec((1, tm, bb.shape[-1]), row),
                  _const_spec(wo.shape)]
    args += [mods, g.reshape(1, d), w1, w3, w2]
    specs += [pl.BlockSpec((1, N_MOD, d), mods_map), _const_spec((1, d)), _const_spec(w1.shape),
              _const_spec(w3.shape), _const_spec(w2.shape)]
    if final_g is not None:
        args.append(final_g.reshape(1, d))
        specs.append(_const_spec((1, d)))
    kern = functools.partial(_ffn_kernel, j=j, tf=256, mix=mix is not None, final=final_g is not None)
    return pl.pallas_call(
        kern,
        grid=(bsz, n // tm),
        in_specs=specs,
        out_specs=pl.BlockSpec((1, tm, d), row),
        out_shape=jax.ShapeDtypeStruct((bsz, n, d), F32),
        compiler_params=_params("parallel", "parallel"),
        name="ffn_half",
    )(*args)


def _even_proj_kernel(x_ref, xp_ref, xn_ref, mods_ref, g_ref, win_ref, cw_ref, gq_ref, wq_ref, gkv_ref, wkv_ref,
                      cos_ref, sin_ref, a_ref, q_ref, k_ref, v_ref, *, n_tok):
    tm = x_ref.shape[1]
    t = pl.program_id(1)
    ms = mods_ref[0]
    xe = jnp.concatenate([xp_ref[0], x_ref[0], xn_ref[0]], axis=0)
    u = _norm_mod(xe, g_ref[...], ms[3:4], ms[4:5]).astype(BF16)
    p = _dot(u, win_ref[...])

    gcv = p[:, D_CONV:2 * D_CONV] * p[:, 2 * D_CONV:3 * D_CONV]
    rows = lax.broadcasted_iota(jnp.int32, (tm + 2 * HALO, 1), 0) + (t * tm - HALO)
    gcv = jnp.where((rows >= 0) & (rows < n_tok), gcv, 0.0)
    prev = pltpu.roll(gcv, 1, 0)[HALO:HALO + tm]
    nxt = pltpu.roll(gcv, tm + 2 * HALO - 1, 0)[HALO:HALO + tm]
    cw = cw_ref[...]
    conv = prev * cw[0:1] + gcv[HALO:HALO + tm] * cw[1:2] + nxt * cw[2:3]
    pm = p[HALO:HALO + tm]
    a_ref[0] = (pm[:, :D_CONV] * conv).astype(BF16)

    cos = cos_ref[...]
    sin = sin_ref[...]
    qa = _dot((_rms(pm[:, Q_OFF:KV_OFF]) * (gq_ref[...] * MLA_SCALE)).astype(BF16), wq_ref[...])
    kv = _dot((_rms(pm[:, KV_OFF:KR_OFF]) * gkv_ref[...]).astype(BF16), wkv_ref[...])
    krb = pm[:, KR_OFF:]
    kr = krb * cos + pltpu.roll(krb, LANES - MLA_ROPE, 1) * sin
    lane = lax.broadcasted_iota(jnp.int32, (tm, LANES), 1)
    low = lane < MLA_NOPE
    for h in range(MLA_HEADS):
        sl = slice(h * LANES, (h + 1) * LANES)
        qh = qa[:, sl]
        q_ref[0, h] = (qh * cos + pltpu.roll(qh, LANES - MLA_ROPE, 1) * sin).astype(BF16)
        kvh = kv[:, sl]
        k_ref[0, h] = jnp.where(low, kvh, kr).astype(BF16)
        if h % 2 == 0:
            vh = jnp.where(low, pltpu.roll(kvh, MLA_NOPE, 1), jnp.where(lane == MLA_V, 1.0, 0.0))
        else:
            vh = jnp.where(low, jnp.where(lane == 0, 1.0, 0.0), kvh)
        v_ref[0, h] = vh.astype(BF16)


def _even_proj(x, mods, g, w_in, conv_w, gq, wq, gkv, wkv, cos, sin, *, tm=512):
    bsz, n, d = x.shape
    tm = min(tm, n)
    nh = tm // HALO
    last = n // HALO - 1
    per_batch = mods.shape[0] > 1
    mods_map = (lambda b, t: (b, 0, 0)) if per_batch else (lambda b, t: (0, 0, 0))
    row = lambda b, t: (b, t, 0)
    head_spec = pl.BlockSpec((1, MLA_HEADS, tm, LANES), lambda b, t: (b, 0, t, 0))
    head_shape = jax.ShapeDtypeStruct((bsz, MLA_HEADS, n, LANES), BF16)
    return pl.pallas_call(
        functools.partial(_even_proj_kernel, n_tok=n),
        grid=(bsz, n // tm),
        in_specs=[
            pl.BlockSpec((1, tm, d), row),
            pl.BlockSpec((1, HALO, d), lambda b, t: (b, jnp.maximum(t * nh - 1, 0), 0)),
            pl.BlockSpec((1, HALO, d), lambda b, t: (b, jnp.minimum((t + 1) * nh, last), 0)),
            pl.BlockSpec((1, N_MOD, d), mods_map),
            _const_spec((1, d)),
            _const_spec(w_in.shape),
            _const_spec(conv_w.shape),
            _const_spec((1, MLA_Q_RANK)),
            _const_spec(wq.shape),
            _const_spec((1, MLA_KV_RANK)),
            _const_spec(wkv.shape),
            pl.BlockSpec((tm, LANES), lambda b, t: (t, 0)),
            pl.BlockSpec((tm, LANES), lambda b, t: (t, 0)),
        ],
        out_specs=[pl.BlockSpec((1, tm, D_CONV), row), head_spec, head_spec, head_spec],
        out_shape=[jax.ShapeDtypeStruct((bsz, n, D_CONV), BF16), head_shape, head_shape, head_shape],
        compiler_params=_params("parallel", "parallel"),
        name="even_proj",
    )(x, x, x, mods, g.reshape(1, d), w_in, conv_w, gq.reshape(1, -1), wq, gkv.reshape(1, -1), wkv, cos, sin)


def _attn_kernel(*refs, seg_chunks, unroll):
    q_ref = refs[0]
    nseg = len(seg_chunks)
    kv_refs = refs[1:1 + 2 * nseg]
    o_ref = refs[1 + 2 * nseg]
    m_ref, acc_ref, s_ref = refs[2 + 2 * nseg:]
    tq = q_ref.shape[2]

    m_ref[...] = jnp.full(m_ref.shape, -jnp.inf, F32)
    acc_ref[...] = jnp.zeros(acc_ref.shape, F32)

    for s, (tk, nk) in enumerate(seg_chunks):
        k_ref, v_ref = kv_refs[2 * s], kv_refs[2 * s + 1]

        def scores(hh, off, k_ref=k_ref, tk=tk):
            s_ref[hh, :, :tk] = _dot_nt(q_ref[0, hh], k_ref[0, hh, pl.ds(off, tk), :])

        def accumulate(hh, off, v_ref=v_ref, tk=tk):
            sc = s_ref[hh, :, :tk]
            m_prev = m_ref[hh]
            m_new = jnp.maximum(m_prev, jnp.max(sc, axis=1, keepdims=True))
            p = jnp.exp(sc - pltpu.repeat(m_new, tk // LANES, 1))
            acc_ref[hh] = (acc_ref[hh] * jnp.exp(m_prev - m_new)
                           + _dot(p.astype(BF16), v_ref[0, hh, pl.ds(off, tk), :]))
            m_ref[hh] = m_new

        def step(c, carry, tk=tk, scores=scores, accumulate=accumulate, prefetch=True):
            off = c * tk if isinstance(c, int) else pl.multiple_of(c * tk, tk)
            scores(1, off)
            accumulate(0, off)
            if prefetch:
                scores(0, pl.multiple_of((c + 1) * tk, tk))
            accumulate(1, off)
            return carry

        scores(0, 0)
        if nk > 1:
            lax.fori_loop(0, nk - 1, step, 0, unroll=unroll)
        step(nk - 1, 0, prefetch=False)

    lane = lax.broadcasted_iota(jnp.int32, (tq, LANES), 1)
    acc_e, acc_o = acc_ref[0], acc_ref[1]
    den_e = jnp.sum(jnp.where(lane == MLA_V, acc_e, 0.0), axis=1, keepdims=True)
    den_o = jnp.sum(jnp.where(lane == 0, acc_o, 0.0), axis=1, keepdims=True)
    o_ref[0] = jnp.where(lane < MLA_V, acc_e / den_e, acc_o / den_o).astype(o_ref.dtype)


def _attention(q, segs, *, tq=512, tk=1024):
    bsz, nh, n, _ = q.shape
    tq = min(tq, n)
    args = [q]
    specs = [pl.BlockSpec((1, 2, tq, LANES), lambda b, h, t: (b, h, t, 0))]
    seg_chunks = []
    for k, v in segs:
        ns = k.shape[2]
        tks = min(tk, ns)
        seg_chunks.append((tks, ns // tks))
        spec = pl.BlockSpec((1, 2, ns, LANES), lambda b, h, t: (b, h, 0, 0))
        args += [k, v]
        specs += [spec, spec]
    return pl.pallas_call(
        functools.partial(_attn_kernel, seg_chunks=tuple(seg_chunks), unroll=True),
        grid=(bsz, nh // 2, n // tq),
        in_specs=specs,
        out_specs=pl.BlockSpec((1, tq, LANES), lambda b, h, t: (b, t, h)),
        out_shape=jax.ShapeDtypeStruct((bsz, n, nh // 2 * LANES), BF16),
        scratch_shapes=[pltpu.VMEM((2, tq, LANES), F32), pltpu.VMEM((2, tq, LANES), F32),
                        pltpu.VMEM((2, tq, max(t for t, _ in seg_chunks)), F32)],
        compiler_params=_params("parallel", "parallel", "parallel"),
        name="mla_attention",
    )(*args)


def _split3(v):
    hi = v.astype(BF16)
    r = v - hi.astype(F32)
    mid = r.astype(BF16)
    lo = (r - mid.astype(F32)).astype(BF16)
    return hi, mid, lo


def _gla_tile(st_ref, lb, z, val, q, reverse, dst):
    n = z.shape[0]
    nch = n // HGRN_CHUNK
    order = list(reversed(range(nch))) if reverse else list(range(nch))
    r = lax.broadcasted_iota(jnp.int32, (n, n), 0)
    c = lax.broadcasted_iota(jnp.int32, (n, n), 1)
    same = (r // HGRN_CHUNK) == (c // HGRN_CHUNK)
    tri = jnp.where(same & ((c >= r) if reverse else (c <= r)), 1.0, 0.0).astype(BF16)

    f = lb + (1.0 - lb) * jax.nn.sigmoid(z)
    kk = 1.0 - f
    hi, mid, lo = _split3(jnp.log(f))
    b = _dot(tri, hi) + _dot(tri, mid) + _dot(tri, lo)
    tot_row = [ci * HGRN_CHUNK + (0 if reverse else HGRN_CHUNK - 1) for ci in range(nch)]
    btot = [b[tr:tr + 1] for tr in tot_row]
    rest = jnp.concatenate([jnp.broadcast_to(bt, (HGRN_CHUNK, bt.shape[1])) for bt in btot], axis=0) - b
    k_end = (kk * jnp.exp(rest)).astype(BF16)
    vb = val.astype(BF16)
    if q is not None:
        q_t = (q * jnp.exp(b)).astype(BF16)
        k_t = (kk * jnp.exp(-b)).astype(BF16)
        rr = lax.broadcasted_iota(jnp.int32, (HGRN_CHUNK, HGRN_CHUNK), 0)
        cc = lax.broadcasted_iota(jnp.int32, (HGRN_CHUNK, HGRN_CHUNK), 1)
        mask = (cc >= rr) if reverse else (cc <= rr)

    def blk(a, ci, h):
        return a[ci * HGRN_CHUNK:(ci + 1) * HGRN_CHUNK, h * HGRN_EXPAND:(h + 1) * HGRN_EXPAND]

    intra, dst_upd = {}, {}
    for ci in order:
        for h in range(HGRN_HEADS):
            if q is not None:
                att = jnp.where(mask, _dot_nt(blk(q_t, ci, h), blk(k_t, ci, h)), 0.0).astype(BF16)
                intra[ci, h] = _dot(att, blk(vb, ci, h))
            dst_upd[ci, h] = _dot_tn(blk(vb, ci, h), blk(k_end, ci, h))
    for ci in order:
        decay = jnp.exp(btot[ci])
        for h in range(HGRN_HEADS):
            st = st_ref[h]
            if q is not None:
                dst[ci * HGRN_CHUNK:(ci + 1) * HGRN_CHUNK, h * HGRN_EXPAND:(h + 1) * HGRN_EXPAND] = (
                    intra[ci, h] + _dot_nt(blk(q_t, ci, h), st.astype(BF16)))
            st_ref[h] = st * decay[:, h * HGRN_EXPAND:(h + 1) * HGRN_EXPAND] + dst_upd[ci, h]


def _hgrn_kernel(*refs, reverse, final):
    refs = list(refs)
    x_ref, xc_ref, mods_ref, modc_ref, g_ref, win_ref, lb_ref = refs[:7]
    refs = refs[7:]
    if final:
        ob_ref, gn_ref, wo_ref = refs[:3]
        refs = refs[3:]
    o_ref = refs.pop(0)
    st_ref = refs.pop(0)
    if final:
        of_ref = refs.pop(0)

    t = pl.program_id(1)
    tm = x_ref.shape[1]
    lb = lb_ref[...]

    @pl.when(t == 0)
    def _():
        st_ref[...] = jnp.zeros(st_ref.shape, F32)
        mc = modc_ref[0]
        u = _norm_mod(xc_ref[0], g_ref[...], mc[3:4], mc[4:5]).astype(BF16)
        p = _dot(u, win_ref[:, D_HGRN:3 * D_HGRN])
        _gla_tile(st_ref, lb, p[:, D_HGRN:], p[:, :D_HGRN], None, reverse, None)

    @pl.when(t > 0)
    def _():
        ms = mods_ref[0]
        x = x_ref[0]
        u = _norm_mod(x, g_ref[...], ms[3:4], ms[4:5]).astype(BF16)
        p = _dot(u, win_ref[...])
        dst = of_ref if final else o_ref.at[0]
        _gla_tile(st_ref, lb, p[:, 2 * D_HGRN:3 * D_HGRN], p[:, D_HGRN:2 * D_HGRN], p[:, :D_HGRN] * HGRN_SCALE,
                  reverse, dst)
        if final:
            o = of_ref[...] + ob_ref[0]
            gn = gn_ref[...]
            ys = [_rms(o[:, h * HGRN_EXPAND:(h + 1) * HGRN_EXPAND]) * gn for h in range(HGRN_HEADS)]
            y = (jnp.concatenate(ys, axis=1) * _silu(p[:, 3 * D_HGRN:])).astype(BF16)
            o_ref[0] = x + ms[5:6] * _dot(y, wo_ref[...])


def _hgrn_pass(x, xc, mods_x, mods_c, g, w_in, lb, *, reverse, final=None, tm=256):
    bsz, n, d = x.shape
    nc = xc.shape[1]
    tm = min(tm, n)
    nt = n // tm
    if reverse:
        xmap = lambda b, t: (b, nt - 1 - jnp.maximum(t - 1, 0), 0)
    else:
        xmap = lambda b, t: (b, jnp.maximum(t - 1, 0), 0)
    args = [x, xc, mods_x, mods_c, g.reshape(1, d), w_in, lb.reshape(1, D_HGRN)]
    specs = [
        pl.BlockSpec((1, tm, d), xmap),
        pl.BlockSpec((1, nc, d), lambda b, t: (b, 0, 0)),
        pl.BlockSpec((1, N_MOD, d), lambda b, t: (b, 0, 0)),
        pl.BlockSpec((1, N_MOD, d), lambda b, t: (0, 0, 0)),
        _const_spec((1, d)),
        _const_spec(w_in.shape),
        _const_spec((1, D_HGRN)),
    ]
    scratch = [pltpu.VMEM((HGRN_HEADS, HGRN_EXPAND, HGRN_EXPAND), F32)]
    if final is not None:
        ob, gn, wo = final
        args += [ob, gn.reshape(1, HGRN_EXPAND), wo]
        specs += [pl.BlockSpec((1, tm, D_HGRN), xmap), _const_spec((1, HGRN_EXPAND)), _const_spec(wo.shape)]
        scratch.append(pltpu.VMEM((tm, D_HGRN), F32))
        out_d = d
    else:
        out_d = D_HGRN
    return pl.pallas_call(
        functools.partial(_hgrn_kernel, reverse=reverse, final=final is not None),
        grid=(bsz, nt + 1),
        in_specs=specs,
        out_specs=pl.BlockSpec((1, tm, out_d), xmap),
        out_shape=jax.ShapeDtypeStruct((bsz, n, out_d), F32),
        scratch_shapes=scratch,
        compiler_params=_params("parallel", "arbitrary"),
        name="hgrn_bwd" if reverse else "hgrn_fwd",
    )(*args)


def _rope_tables(n):
    rows = n // GRID_W
    row = jnp.repeat(jnp.arange(rows, dtype=jnp.int32), GRID_W).astype(F32)
    col = jnp.tile(jnp.arange(GRID_W, dtype=jnp.int32), rows).astype(F32)
    n_freq = MLA_ROPE // 4
    inv = ROPE_BASE ** (-jnp.arange(n_freq, dtype=F32) / n_freq)
    ar, ac = row[:, None] * inv, col[:, None] * inv
    cos = jnp.concatenate([jnp.cos(ar), jnp.cos(ar), jnp.cos(ac), jnp.cos(ac)], axis=1)
    sin = jnp.concatenate([-jnp.sin(ar), jnp.sin(ar), -jnp.sin(ac), jnp.sin(ac)], axis=1)
    ones = jnp.ones((n, MLA_NOPE), F32)
    zeros = jnp.zeros((n, MLA_ROPE), F32)
    return (jnp.concatenate([ones, cos, zeros], axis=1),
            jnp.concatenate([jnp.zeros((n, MLA_NOPE), F32), sin, zeros], axis=1))


def _swap_rope_cols(w):
    q = MLA_ROPE // 4
    return jnp.concatenate([w[:, q:2 * q], w[:, :q], w[:, 3 * q:], w[:, 2 * q:3 * q]], axis=1)


def _even_weights(w_in, w_uq):
    kr = w_in[:, KR_OFF:]
    w_in = jnp.concatenate([w_in[:, :KR_OFF], jnp.zeros((w_in.shape[0], MLA_NOPE), w_in.dtype), kr,
                            _swap_rope_cols(kr)], axis=1)
    wq = w_uq.reshape(MLA_Q_RANK, MLA_HEADS, MLA_NOPE + MLA_ROPE)
    blocks = []
    for h in range(MLA_HEADS):
        rope = wq[:, h, MLA_NOPE:]
        blocks += [wq[:, h, :MLA_NOPE], rope, _swap_rope_cols(rope)]
    return w_in.astype(BF16), jnp.concatenate(blocks, axis=1).astype(BF16)


def kernel(x, c, ctx, c_ctx, ada_w, ada_b, norm_g, ffn_w1, ffn_w3, ffn_w2, even_w_in, even_conv_w, mla_q_norm_g,
           mla_w_uq, mla_kv_norm_g, mla_w_ukv, even_w_out, odd_w_in, hgrn_lb_logits, hgrn_g_norm_g, odd_w_out,
           final_norm_g):
    bsz, n, d = x.shape
    depth = ada_w.shape[0]
    n_ctx = ctx.shape[1]

    cond = jnp.concatenate([c, c_ctx[None], jnp.zeros((8 - bsz - 1, d), F32)], axis=0)
    mods = _ada_mods(cond, ada_w, ada_b).reshape(depth, 8, N_MOD, d)

    lb_p = jax.nn.softmax(hgrn_lb_logits.astype(F32), axis=0)
    lb_table = jnp.cumsum(lb_p, axis=0) - lb_p[0]

    w1, w3, w2 = ffn_w1.astype(BF16), ffn_w3.astype(BF16), ffn_w2.astype(BF16)
    cos_x, sin_x = _rope_tables(n)
    cos_c = jnp.concatenate([jnp.ones((n_ctx, MLA_NOPE + MLA_ROPE), F32), jnp.zeros((n_ctx, MLA_ROPE), F32)], axis=1)
    sin_c = jnp.zeros((n_ctx, LANES), F32)

    h = ctx
    for l in range(depth):
        need_ctx_out = l < depth - 1
        mx, mc = mods[l, :bsz], mods[l, bsz:bsz + 1]
        last = l == depth - 1
        x = _ffn_half(x, mx, norm_g[l, 0], w1[l, 0], w3[l, 0], w2[l, 0], j=0)
        h = _ffn_half(h, mc, norm_g[l, 0], w1[l, 0], w3[l, 0], w2[l, 0], j=0)
        if l % 2 == 0:
            e = l // 2
            w_in, wq = _even_weights(even_w_in[e], mla_w_uq[e])
            wkv, wo = mla_w_ukv[e].astype(BF16), even_w_out[e].astype(BF16)
            proj = functools.partial(_even_proj, g=norm_g[l, 1], w_in=w_in, conv_w=even_conv_w[e],
                                     gq=mla_q_norm_g[e], wq=wq, gkv=mla_kv_norm_g[e], wkv=wkv)
            a_x, q_x, k_x, v_x = proj(x, mx, cos=cos_x, sin=sin_x)
            a_c, q_c, k_c, v_c = proj(h, mc, cos=cos_c, sin=sin_c)
            b_x = _attention(q_x, [(k_c, v_c), (k_x, v_x)])
            x = _ffn_half(x, mx, norm_g[l, 2], w1[l, 1], w3[l, 1], w2[l, 1], j=2, mix=(a_x, b_x, wo),
                          final_g=final_norm_g if last else None)
            if need_ctx_out:
                b_c = _attention(q_c, [(k_c, v_c)])
                h = _ffn_half(h, mc, norm_g[l, 2], w1[l, 1], w3[l, 1], w2[l, 1], j=2, mix=(a_c, b_c, wo))
        else:
            if need_ctx_out:
                raise NotImplementedError("context output of an HGRN2 layer is only needed for depth > 2")
            o = l // 2
            w_in = odd_w_in[o].astype(BF16)
            lb = lb_table[l]
            w_bwd = jnp.concatenate([w_in[:, :2 * D_HGRN], w_in[:, 3 * D_HGRN:4 * D_HGRN]], axis=1)
            w_fwd = jnp.concatenate([w_in[:, :3 * D_HGRN], w_in[:, 4 * D_HGRN:]], axis=1)
            o_b = _hgrn_pass(x, h, mx, mc, norm_g[l, 1], w_bwd, lb[1], reverse=True)
            x = _hgrn_pass(x, h, mx, mc, norm_g[l, 1], w_fwd, lb[0], reverse=False,
                           final=(o_b, hgrn_g_norm_g[o], odd_w_out[o].astype(BF16)))
            x = _ffn_half(x, mx, norm_g[l, 2], w1[l, 1], w3[l, 1], w2[l, 1], j=2,
                          final_g=final_norm_g if last else None)
    return x
```

```python
import functools

import jax
import jax.numpy as jnp
from jax import lax
from jax.experimental import pallas as pl
from jax.experimental.pallas import tpu as pltpu

F32 = jnp.float32
BF16 = jnp.bfloat16

EPS = 1e-6
N_MOD = 9
GRID_W = 64
ROPE_BASE = 10000.0
LANES = 128

D_CONV = 512
MLA_HEADS = 8
MLA_NOPE = 64
MLA_ROPE = 32
MLA_V = 64
MLA_Q_RANK = 256
MLA_KV_RANK = 128
MLA_SCALE = (MLA_NOPE + MLA_ROPE) ** -0.5
LOG2_E = 1.4426950408889634
Q_OFF = 3 * D_CONV
KV_OFF = Q_OFF + MLA_Q_RANK
KR_OFF = KV_OFF + MLA_KV_RANK
D_EVEN_PROJ = KR_OFF + LANES
HALO = 16

HGRN_HEADS = 8
HGRN_EXPAND = 128
D_HGRN = HGRN_HEADS * HGRN_EXPAND
HGRN_SCALE = HGRN_EXPAND ** -0.5
HGRN_CHUNK = 64

VMEM_LIMIT = 56 * 1024 * 1024


def _dot(a, b):
    return jnp.dot(a, b, preferred_element_type=F32)


def _dot_nt(a, b):
    return lax.dot_general(a, b, (((1,), (1,)), ((), ())), preferred_element_type=F32)


def _dot_tn(a, b):
    return lax.dot_general(a, b, (((0,), (0,)), ((), ())), preferred_element_type=F32)


def _silu(v):
    return v * jax.nn.sigmoid(v)


def _rms(v):
    return v * lax.rsqrt(jnp.mean(v * v, axis=-1, keepdims=True) + EPS)


def _norm_mod(x, g, shift, scale):
    return _rms(x) * (g * (1.0 + scale)) + shift


def _const_spec(shape):
    nd = len(shape)
    return pl.BlockSpec(shape, lambda *_: (0,) * nd, pipeline_mode=pl.Buffered(1))


def _params(*sem):
    return pltpu.CompilerParams(dimension_semantics=sem, vmem_limit_bytes=VMEM_LIMIT)


def _ada_kernel(cond_ref, w_ref, b_ref, o_ref):
    s = _silu(cond_ref[...]).astype(BF16)
    o_ref[0] = _dot(s, w_ref[0].astype(BF16)) + b_ref[0]


def _ada_mods(cond, ada_w, ada_b):
    depth, d, n = ada_w.shape
    rows = cond.shape[0]
    tn = 1024
    return pl.pallas_call(
        _ada_kernel,
        grid=(depth, n // tn),
        in_specs=[
            pl.BlockSpec((rows, d), lambda l, j: (0, 0)),
            pl.BlockSpec((1, d, tn), lambda l, j: (l, 0, j)),
            pl.BlockSpec((1, 1, tn), lambda l, j: (l, 0, j)),
        ],
        out_specs=pl.BlockSpec((1, rows, tn), lambda l, j: (l, 0, j)),
        out_shape=jax.ShapeDtypeStruct((depth, rows, n), F32),
        compiler_params=_params("arbitrary", "arbitrary"),
        name="ada_mods",
    )(cond, ada_w, ada_b.reshape(depth, 1, n))


def _ffn_kernel(*refs, j, tf, mix, final):
    refs = list(refs)
    x_ref = refs.pop(0)
    if mix:
        a_ref, b_ref, wo_ref = refs.pop(0), refs.pop(0), refs.pop(0)
    mods_ref, g_ref, w1_ref, w3_ref, w2_ref = refs[:5]
    refs = refs[5:]
    if final:
        gf_ref = refs.pop(0)
    o_ref = refs.pop(0)

    x = x_ref[0]
    ms = mods_ref[0]
    if mix:
        dc = a_ref.shape[-1]
        y = _dot(a_ref[0], wo_ref[:dc, :]) + _dot(b_ref[0], wo_ref[dc:, :])
        x = x + ms[5:6] * y
    u = _norm_mod(x, g_ref[...], ms[3 * j:3 * j + 1], ms[3 * j + 1:3 * j + 2]).astype(BF16)
    acc = jnp.zeros(x.shape, F32)
    for c in range(w1_ref.shape[1] // tf):
        sl = slice(c * tf, (c + 1) * tf)
        h = (_silu(_dot(u, w1_ref[:, sl])) * _dot(u, w3_ref[:, sl])).astype(BF16)
        acc = acc + _dot(h, w2_ref[sl, :])
    out = x + (0.5 * ms[3 * j + 2:3 * j + 3]) * acc
    if final:
        out = _rms(out) * gf_ref[...]
    o_ref[0] = out


def _ffn_half(x, mods, g, w1, w3, w2, *, j, mix=None, final_g=None, tm=512):
    bsz, n, d = x.shape
    f = w1.shape[1]
    tm = min(tm, n)
    per_batch = mods.shape[0] > 1
    mods_map = (lambda b, t: (b, 0, 0)) if per_batch else (lambda b, t: (0, 0, 0))
    row = lambda b, t: (b, t, 0)
    args = [x]
    specs = [pl.BlockSpec((1, tm, d), row)]
    if mix is not None:
        a, bb, wo = mix
        args += [a, bb, wo]
        specs += [pl.BlockSpec((1, tm, a.shape[-1]), row), pl.BlockSpec((1, tm, bb.shape[-1]), row),
                  _const_spec(wo.shape)]
    args += [mods, g.reshape(1, d), w1, w3, w2]
    specs += [pl.BlockSpec((1, N_MOD, d), mods_map), _const_spec((1, d)), _const_spec(w1.shape),
              _const_spec(w3.shape), _const_spec(w2.shape)]
    if final_g is not None:
        args.append(final_g.reshape(1, d))
        specs.append(_const_spec((1, d)))
    kern = functools.partial(_ffn_kernel, j=j, tf=256, mix=mix is not None, final=final_g is not None)
    return pl.pallas_call(
        kern,
        grid=(bsz, n // tm),
        in_specs=specs,
        out_specs=pl.BlockSpec((1, tm, d), row),
        out_shape=jax.ShapeDtypeStruct((bsz, n, d), F32),
        compiler_params=_params("parallel", "parallel"),
        name="ffn_half",
    )(*args)


def _even_proj_kernel(x_ref, xp_ref, xn_ref, mods_ref, g_ref, win_ref, cw_ref, gq_ref, wq_ref, gkv_ref, wkv_ref,
                      cos_ref, sin_ref, a_ref, q_ref, k_ref, v_ref, *, n_tok):
    tm = x_ref.shape[1]
    t = pl.program_id(1)
    ms = mods_ref[0]
    xe = jnp.concatenate([xp_ref[0], x_ref[0], xn_ref[0]], axis=0)
    u = _norm_mod(xe, g_ref[...], ms[3:4], ms[4:5]).astype(BF16)
    p = _dot(u, win_ref[...])

    gcv = p[:, D_CONV:2 * D_CONV] * p[:, 2 * D_CONV:3 * D_CONV]
    rows = lax.broadcasted_iota(jnp.int32, (tm + 2 * HALO, 1), 0) + (t * tm - HALO)
    gcv = jnp.where((rows >= 0) & (rows < n_tok), gcv, 0.0)
    prev = pltpu.roll(gcv, 1, 0)[HALO:HALO + tm]
    nxt = pltpu.roll(gcv, tm + 2 * HALO - 1, 0)[HALO:HALO + tm]
    cw = cw_ref[...]
    conv = prev * cw[0:1] + gcv[HALO:HALO + tm] * cw[1:2] + nxt * cw[2:3]
    pm = p[HALO:HALO + tm]
    a_ref[0] = (pm[:, :D_CONV] * conv).astype(BF16)

    cos = cos_ref[...]
    sin = sin_ref[...]
    qa = _dot((_rms(pm[:, Q_OFF:KV_OFF]) * (gq_ref[...] * (MLA_SCALE * LOG2_E))).astype(BF16), wq_ref[...])
    kv = _dot((_rms(pm[:, KV_OFF:KR_OFF]) * gkv_ref[...]).astype(BF16), wkv_ref[...])
    krb = pm[:, KR_OFF:]
    kr = krb * cos + pltpu.roll(krb, LANES - MLA_ROPE, 1) * sin
    lane = lax.broadcasted_iota(jnp.int32, (tm, LANES), 1)
    low = lane < MLA_NOPE
    for h in range(MLA_HEADS):
        sl = slice(h * LANES, (h + 1) * LANES)
        qh = qa[:, sl]
        q_ref[0, h] = (qh * cos + pltpu.roll(qh, LANES - MLA_ROPE, 1) * sin).astype(BF16)
        kvh = kv[:, sl]
        k_ref[0, h] = jnp.where(low, kvh, kr).astype(BF16)
        if h % 2 == 0:
            vh = jnp.where(low, pltpu.roll(kvh, MLA_NOPE, 1), jnp.where(lane == MLA_V, 1.0, 0.0))
        else:
            vh = jnp.where(low, jnp.where(lane == 0, 1.0, 0.0), kvh)
        v_ref[0, h] = vh.astype(BF16)


def _even_proj(x, mods, g, w_in, conv_w, gq, wq, gkv, wkv, cos, sin, *, tm=512):
    bsz, n, d = x.shape
    tm = min(tm, n)
    nh = tm // HALO
    last = n // HALO - 1
    per_batch = mods.shape[0] > 1
    mods_map = (lambda b, t: (b, 0, 0)) if per_batch else (lambda b, t: (0, 0, 0))
    row = lambda b, t: (b, t, 0)
    head_spec = pl.BlockSpec((1, MLA_HEADS, tm, LANES), lambda b, t: (b, 0, t, 0))
    head_shape = jax.ShapeDtypeStruct((bsz, MLA_HEADS, n, LANES), BF16)
    return pl.pallas_call(
        functools.partial(_even_proj_kernel, n_tok=n),
        grid=(bsz, n // tm),
        in_specs=[
            pl.BlockSpec((1, tm, d), row),
            pl.BlockSpec((1, HALO, d), lambda b, t: (b, jnp.maximum(t * nh - 1, 0), 0)),
            pl.BlockSpec((1, HALO, d), lambda b, t: (b, jnp.minimum((t + 1) * nh, last), 0)),
            pl.BlockSpec((1, N_MOD, d), mods_map),
            _const_spec((1, d)),
            _const_spec(w_in.shape),
            _const_spec(conv_w.shape),
            _const_spec((1, MLA_Q_RANK)),
            _const_spec(wq.shape),
            _const_spec((1, MLA_KV_RANK)),
            _const_spec(wkv.shape),
            pl.BlockSpec((tm, LANES), lambda b, t: (t, 0)),
            pl.BlockSpec((tm, LANES), lambda b, t: (t, 0)),
        ],
        out_specs=[pl.BlockSpec((1, tm, D_CONV), row), head_spec, head_spec, head_spec],
        out_shape=[jax.ShapeDtypeStruct((bsz, n, D_CONV), BF16), head_shape, head_shape, head_shape],
        compiler_params=_params("parallel", "parallel"),
        name="even_proj",
    )(x, x, x, mods, g.reshape(1, d), w_in, conv_w, gq.reshape(1, -1), wq, gkv.reshape(1, -1), wkv, cos, sin)


def _attn_kernel(*refs, seg_chunks, unroll):
    q_ref = refs[0]
    nseg = len(seg_chunks)
    kv_refs = refs[1:1 + 2 * nseg]
    o_ref = refs[1 + 2 * nseg]
    m_ref, acc_ref, s_ref = refs[2 + 2 * nseg:]
    tq = q_ref.shape[2]

    m_ref[...] = jnp.full(m_ref.shape, -jnp.inf, F32)
    acc_ref[...] = jnp.zeros(acc_ref.shape, F32)

    for s, (tk, nk) in enumerate(seg_chunks):
        k_ref, v_ref = kv_refs[2 * s], kv_refs[2 * s + 1]

        def scores(hh, off, k_ref=k_ref, tk=tk):
            s_ref[hh, :, :tk] = _dot_nt(q_ref[0, hh], k_ref[0, hh, pl.ds(off, tk), :])

        def accumulate(hh, off, v_ref=v_ref, tk=tk):
            sc = s_ref[hh, :, :tk]
            m_prev = m_ref[hh]
            m_new = jnp.maximum(m_prev, jnp.max(sc, axis=1, keepdims=True))
            p = jnp.exp2(sc - jnp.concatenate([m_new] * (tk // LANES), axis=1))
            acc_ref[hh] = (acc_ref[hh] * jnp.exp2(m_prev - m_new)
                           + _dot(p.astype(BF16), v_ref[0, hh, pl.ds(off, tk), :]))
            m_ref[hh] = m_new

        def step(c, carry, tk=tk, scores=scores, accumulate=accumulate, prefetch=True):
            off = c * tk if isinstance(c, int) else pl.multiple_of(c * tk, tk)
            scores(1, off)
            accumulate(0, off)
            if prefetch:
                scores(0, pl.multiple_of((c + 1) * tk, tk))
            accumulate(1, off)
            return carry

        scores(0, 0)
        if nk > 1:
            lax.fori_loop(0, nk - 1, step, 0, unroll=unroll)
        step(nk - 1, 0, prefetch=False)

    lane = lax.broadcasted_iota(jnp.int32, (tq, LANES), 1)
    acc_e, acc_o = acc_ref[0], acc_ref[1]
    den_e = jnp.sum(jnp.where(lane == MLA_V, acc_e, 0.0), axis=1, keepdims=True)
    den_o = jnp.sum(jnp.where(lane == 0, acc_o, 0.0), axis=1, keepdims=True)
    o_ref[0] = jnp.where(lane < MLA_V, acc_e / den_e, acc_o / den_o).astype(o_ref.dtype)


def _attention(q, segs, *, tq=512, tk=1024):
    bsz, nh, n, _ = q.shape
    tq = min(tq, n)
    args = [q]
    specs = [pl.BlockSpec((1, 2, tq, LANES), lambda b, h, t: (b, h, t, 0))]
    seg_chunks = []
    for k, v in segs:
        ns = k.shape[2]
        tks = min(tk, ns)
        seg_chunks.append((tks, ns // tks))
        spec = pl.BlockSpec((1, 2, ns, LANES), lambda b, h, t: (b, h, 0, 0))
        args += [k, v]
        specs += [spec, spec]
    return pl.pallas_call(
        functools.partial(_attn_kernel, seg_chunks=tuple(seg_chunks), unroll=True),
        grid=(bsz, nh // 2, n // tq),
        in_specs=specs,
        out_specs=pl.BlockSpec((1, tq, LANES), lambda b, h, t: (b, t, h)),
        out_shape=jax.ShapeDtypeStruct((bsz, n, nh // 2 * LANES), BF16),
        scratch_shapes=[pltpu.VMEM((2, tq, LANES), F32), pltpu.VMEM((2, tq, LANES), F32),
                        pltpu.VMEM((2, tq, max(t for t, _ in seg_chunks)), F32)],
        compiler_params=_params("parallel", "parallel", "parallel"),
        name="mla_attention",
    )(*args)


def _split3(v):
    hi = v.astype(BF16)
    r = v - hi.astype(F32)
    mid = r.astype(BF16)
    lo = (r - mid.astype(F32)).astype(BF16)
    return hi, mid, lo


def _gla_rows(st_ref, lb, z, val, q, reverse, dst, group=256):
    n = z.shape[0]
    starts = list(range(0, n, group))
    starts = list(reversed(starts)) if reverse else starts
    preps = [_gla_prep(lb, z[r0:r0 + group], val[r0:r0 + group], None if q is None else q[r0:r0 + group], reverse)
             for r0 in starts]
    for r0, prep in zip(starts, preps):
        _gla_scan(st_ref, prep, reverse, None if dst is None else dst.at[r0:r0 + group])


def _gla_prep(lb, z, val, q, reverse):
    n = z.shape[0]
    nch = n // HGRN_CHUNK
    r = lax.broadcasted_iota(jnp.int32, (n, n), 0)
    c = lax.broadcasted_iota(jnp.int32, (n, n), 1)
    same = (r // HGRN_CHUNK) == (c // HGRN_CHUNK)
    tri = jnp.where(same & ((c >= r) if reverse else (c <= r)), 1.0, 0.0).astype(BF16)

    f = lb + (1.0 - lb) * jax.nn.sigmoid(z)
    kk = 1.0 - f
    hi, mid, lo = _split3(jnp.log(f))
    b = _dot(tri, hi) + _dot(tri, mid) + _dot(tri, lo)
    tot_row = [ci * HGRN_CHUNK + (0 if reverse else HGRN_CHUNK - 1) for ci in range(nch)]
    btot = [b[tr:tr + 1] for tr in tot_row]
    rest = jnp.concatenate([jnp.broadcast_to(bt, (HGRN_CHUNK, bt.shape[1])) for bt in btot], axis=0) - b
    k_end = (kk * jnp.exp(rest)).astype(BF16)
    vb = val.astype(BF16)
    if q is None:
        return None, None, k_end, vb, btot
    return (q * jnp.exp(b)).astype(BF16), (kk * jnp.exp(-b)).astype(BF16), k_end, vb, btot


def _gla_scan(st_ref, prep, reverse, dst):
    q_t, k_t, k_end, vb, btot = prep
    nch = len(btot)
    order = list(reversed(range(nch))) if reverse else list(range(nch))
    if q_t is not None:
        rr = lax.broadcasted_iota(jnp.int32, (HGRN_CHUNK, HGRN_CHUNK), 0)
        cc = lax.broadcasted_iota(jnp.int32, (HGRN_CHUNK, HGRN_CHUNK), 1)
        mask = (cc >= rr) if reverse else (cc <= rr)

    def blk(a, ci, h):
        return a[ci * HGRN_CHUNK:(ci + 1) * HGRN_CHUNK, h * HGRN_EXPAND:(h + 1) * HGRN_EXPAND]

    intra, dst_upd = {}, {}
    for ci in order:
        for h in range(HGRN_HEADS):
            if q_t is not None:
                att = jnp.where(mask, _dot_nt(blk(q_t, ci, h), blk(k_t, ci, h)), 0.0).astype(BF16)
                intra[ci, h] = _dot(att, blk(vb, ci, h))
            dst_upd[ci, h] = _dot_tn(blk(vb, ci, h), blk(k_end, ci, h))
    for ci in order:
        decay = jnp.exp(btot[ci])
        for h in range(HGRN_HEADS):
            st = st_ref[h]
            if q_t is not None:
                dst[ci * HGRN_CHUNK:(ci + 1) * HGRN_CHUNK, h * HGRN_EXPAND:(h + 1) * HGRN_EXPAND] = (
                    intra[ci, h] + _dot_nt(blk(q_t, ci, h), st.astype(BF16)))
            st_ref[h] = st * decay[:, h * HGRN_EXPAND:(h + 1) * HGRN_EXPAND] + dst_upd[ci, h]


def _hgrn_bwd_kernel(x_ref, xc_ref, mods_ref, modc_ref, g_ref, win_ref, lb_ref,
                     ob_ref, q_ref, i_ref, zf_ref, gate_ref, st_ref):
    t = pl.program_id(1)
    lb = lb_ref[...]

    @pl.when(t == 0)
    def _():
        st_ref[...] = jnp.zeros(st_ref.shape, F32)
        mc = modc_ref[0]
        u = _norm_mod(xc_ref[0], g_ref[...], mc[3:4], mc[4:5]).astype(BF16)
        _gla_rows(st_ref, lb, _dot(u, win_ref[:, 3 * D_HGRN:4 * D_HGRN]), _dot(u, win_ref[:, D_HGRN:2 * D_HGRN]),
                  None, True, None)

    @pl.when(t > 0)
    def _():
        ms = mods_ref[0]
        u = _norm_mod(x_ref[0], g_ref[...], ms[3:4], ms[4:5]).astype(BF16)
        p = _dot(u, win_ref[...])
        q_ref[0] = p[:, :D_HGRN]
        i_ref[0] = p[:, D_HGRN:2 * D_HGRN]
        zf_ref[0] = p[:, 2 * D_HGRN:3 * D_HGRN]
        gate_ref[0] = p[:, 4 * D_HGRN:]
        _gla_rows(st_ref, lb, p[:, 3 * D_HGRN:4 * D_HGRN], p[:, D_HGRN:2 * D_HGRN], p[:, :D_HGRN] * HGRN_SCALE,
                  True, ob_ref.at[0])


def _hgrn_fwd_kernel(x_ref, xc_ref, mods_ref, modc_ref, g_ref, wc_ref, lb_ref, q_ref, i_ref, zf_ref, gate_ref,
                     ob_ref, gn_ref, wo_ref, o_ref, st_ref, of_ref):
    t = pl.program_id(1)
    lb = lb_ref[...]

    @pl.when(t == 0)
    def _():
        st_ref[...] = jnp.zeros(st_ref.shape, F32)
        mc = modc_ref[0]
        u = _norm_mod(xc_ref[0], g_ref[...], mc[3:4], mc[4:5]).astype(BF16)
        p = _dot(u, wc_ref[...])
        _gla_rows(st_ref, lb, p[:, D_HGRN:], p[:, :D_HGRN], None, False, None)

    @pl.when(t > 0)
    def _():
        _gla_rows(st_ref, lb, zf_ref[0], i_ref[0], q_ref[0] * HGRN_SCALE, False, of_ref)
        o = of_ref[...] + ob_ref[0]
        gn = gn_ref[...]
        ys = [_rms(o[:, h * HGRN_EXPAND:(h + 1) * HGRN_EXPAND]) * gn for h in range(HGRN_HEADS)]
        y = (jnp.concatenate(ys, axis=1) * _silu(gate_ref[0])).astype(BF16)
        o_ref[0] = x_ref[0] + mods_ref[0][5:6] * _dot(y, wo_ref[...])


def _hgrn_mixer(x, xc, mods_x, mods_c, g, w_in, lb, gn, wo, *, tm=512):
    bsz, n, d = x.shape
    nc = xc.shape[1]
    tm = min(tm, n)
    nt = n // tm
    bmap = lambda b, t: (b, nt - 1 - jnp.maximum(t - 1, 0), 0)
    fmap = lambda b, t: (b, jnp.maximum(t - 1, 0), 0)
    common = [
        pl.BlockSpec((1, nc, d), lambda b, t: (b, 0, 0)),
        pl.BlockSpec((1, N_MOD, d), lambda b, t: (b, 0, 0)),
        pl.BlockSpec((1, N_MOD, d), lambda b, t: (0, 0, 0)),
        _const_spec((1, d)),
    ]
    state = pltpu.VMEM((HGRN_HEADS, HGRN_EXPAND, HGRN_EXPAND), F32)
    tile_shape = jax.ShapeDtypeStruct((bsz, n, D_HGRN), F32)
    o_b, q, i, zf, gate = pl.pallas_call(
        _hgrn_bwd_kernel,
        grid=(bsz, nt + 1),
        in_specs=[pl.BlockSpec((1, tm, d), bmap)] + common + [_const_spec(w_in.shape), _const_spec((1, D_HGRN))],
        out_specs=[pl.BlockSpec((1, tm, D_HGRN), bmap)] * 5,
        out_shape=[tile_shape] * 5,
        scratch_shapes=[state],
        compiler_params=_params("parallel", "arbitrary"),
        name="hgrn_bwd",
    )(x, xc, mods_x, mods_c, g.reshape(1, d), w_in, lb[1].reshape(1, D_HGRN))
    ftile = pl.BlockSpec((1, tm, D_HGRN), fmap)
    w_ctx = w_in[:, D_HGRN:3 * D_HGRN]
    return pl.pallas_call(
        _hgrn_fwd_kernel,
        grid=(bsz, nt + 1),
        in_specs=[pl.BlockSpec((1, tm, d), fmap)] + common + [_const_spec(w_ctx.shape), _const_spec((1, D_HGRN))]
        + [ftile] * 5 + [_const_spec((1, HGRN_EXPAND)), _const_spec(wo.shape)],
        out_specs=pl.BlockSpec((1, tm, d), fmap),
        out_shape=jax.ShapeDtypeStruct((bsz, n, d), F32),
        scratch_shapes=[state, pltpu.VMEM((tm, D_HGRN), F32)],
        compiler_params=_params("parallel", "arbitrary"),
        name="hgrn_fwd",
    )(x, xc, mods_x, mods_c, g.reshape(1, d), w_ctx, lb[0].reshape(1, D_HGRN), q, i, zf, gate, o_b,
      gn.reshape(1, HGRN_EXPAND), wo)


def _rope_tables(n):
    rows = n // GRID_W
    n_freq = MLA_ROPE // 4
    inv = ROPE_BASE ** (-jnp.arange(n_freq, dtype=F32) / n_freq)
    ar = jnp.arange(rows, dtype=jnp.int32).astype(F32)[:, None] * inv
    ac = jnp.arange(GRID_W, dtype=jnp.int32).astype(F32)[:, None] * inv

    def per_token(tab_r, tab_c):
        tr = jnp.broadcast_to(tab_r[:, None, :], (rows, GRID_W, n_freq)).reshape(n, n_freq)
        tc = jnp.broadcast_to(tab_c[None, :, :], (rows, GRID_W, n_freq)).reshape(n, n_freq)
        return tr, tc

    cr, cc = per_token(jnp.cos(ar), jnp.cos(ac))
    sr, sc = per_token(jnp.sin(ar), jnp.sin(ac))
    cos = jnp.concatenate([cr, cr, cc, cc], axis=1)
    sin = jnp.concatenate([-sr, sr, -sc, sc], axis=1)
    ones = jnp.ones((n, MLA_NOPE), F32)
    zeros = jnp.zeros((n, MLA_ROPE), F32)
    return (jnp.concatenate([ones, cos, zeros], axis=1),
            jnp.concatenate([jnp.zeros((n, MLA_NOPE), F32), sin, zeros], axis=1))


def _swap_rope_cols(w):
    q = MLA_ROPE // 4
    return jnp.concatenate([w[:, q:2 * q], w[:, :q], w[:, 3 * q:], w[:, 2 * q:3 * q]], axis=1)


def _even_weights(w_in, w_uq):
    kr = w_in[:, KR_OFF:]
    w_in = jnp.concatenate([w_in[:, :KR_OFF], jnp.zeros((w_in.shape[0], MLA_NOPE), w_in.dtype), kr,
                            _swap_rope_cols(kr)], axis=1)
    wq = w_uq.reshape(MLA_Q_RANK, MLA_HEADS, MLA_NOPE + MLA_ROPE)
    blocks = []
    for h in range(MLA_HEADS):
        rope = wq[:, h, MLA_NOPE:]
        blocks += [wq[:, h, :MLA_NOPE], rope, _swap_rope_cols(rope)]
    return w_in.astype(BF16), jnp.concatenate(blocks, axis=1).astype(BF16)


def kernel(x, c, ctx, c_ctx, ada_w, ada_b, norm_g, ffn_w1, ffn_w3, ffn_w2, even_w_in, even_conv_w, mla_q_norm_g,
           mla_w_uq, mla_kv_norm_g, mla_w_ukv, even_w_out, odd_w_in, hgrn_lb_logits, hgrn_g_norm_g, odd_w_out,
           final_norm_g):
    bsz, n, d = x.shape
    depth = ada_w.shape[0]
    n_ctx = ctx.shape[1]

    cond = jnp.concatenate([c, c_ctx[None], jnp.zeros((8 - bsz - 1, d), F32)], axis=0)
    mods = _ada_mods(cond, ada_w, ada_b).reshape(depth, 8, N_MOD, d)

    lb_p = jax.nn.softmax(hgrn_lb_logits.astype(F32), axis=0)
    lb_table = jnp.cumsum(lb_p, axis=0) - lb_p[0]

    w1, w3, w2 = ffn_w1.astype(BF16), ffn_w3.astype(BF16), ffn_w2.astype(BF16)
    cos_x, sin_x = _rope_tables(n)
    cos_c = jnp.concatenate([jnp.ones((n_ctx, MLA_NOPE + MLA_ROPE), F32), jnp.zeros((n_ctx, MLA_ROPE), F32)], axis=1)
    sin_c = jnp.zeros((n_ctx, LANES), F32)

    h = ctx
    for l in range(depth):
        need_ctx_out = l < depth - 1
        mx, mc = mods[l, :bsz], mods[l, bsz:bsz + 1]
        last = l == depth - 1
        x = _ffn_half(x, mx, norm_g[l, 0], w1[l, 0], w3[l, 0], w2[l, 0], j=0)
        h = _ffn_half(h, mc, norm_g[l, 0], w1[l, 0], w3[l, 0], w2[l, 0], j=0)
        if l % 2 == 0:
            e = l // 2
            w_in, wq = _even_weights(even_w_in[e], mla_w_uq[e])
            wkv, wo = mla_w_ukv[e].astype(BF16), even_w_out[e].astype(BF16)
            proj = functools.partial(_even_proj, g=norm_g[l, 1], w_in=w_in, conv_w=even_conv_w[e],
                                     gq=mla_q_norm_g[e], wq=wq, gkv=mla_kv_norm_g[e], wkv=wkv)
            a_x, q_x, k_x, v_x = proj(x, mx, cos=cos_x, sin=sin_x)
            a_c, q_c, k_c, v_c = proj(h, mc, cos=cos_c, sin=sin_c)
            b_x = _attention(q_x, [(k_c, v_c), (k_x, v_x)])
            x = _ffn_half(x, mx, norm_g[l, 2], w1[l, 1], w3[l, 1], w2[l, 1], j=2, mix=(a_x, b_x, wo),
                          final_g=final_norm_g if last else None)
            if need_ctx_out:
                b_c = _attention(q_c, [(k_c, v_c)])
                h = _ffn_half(h, mc, norm_g[l, 2], w1[l, 1], w3[l, 1], w2[l, 1], j=2, mix=(a_c, b_c, wo))
        else:
            if need_ctx_out:
                raise NotImplementedError("context output of an HGRN2 layer is only needed for depth > 2")
            o = l // 2
            x = _hgrn_mixer(x, h, mx, mc, norm_g[l, 1], odd_w_in[o].astype(BF16), lb_table[l], hgrn_g_norm_g[o],
                            odd_w_out[o].astype(BF16))
            x = _ffn_half(x, mx, norm_g[l, 2], w1[l, 1], w3[l, 1], w2[l, 1], j=2,
                          final_g=final_norm_g if last else None)
    return x
```

```python
import functools

import jax
import jax.numpy as jnp
from jax import lax
from jax.experimental import pallas as pl
from jax.experimental.pallas import tpu as pltpu

F32 = jnp.float32
BF16 = jnp.bfloat16

EPS = 1e-6
N_MOD = 9
GRID_W = 64
ROPE_BASE = 10000.0
LANES = 128

D_CONV = 512
MLA_HEADS = 8
MLA_NOPE = 64
MLA_ROPE = 32
MLA_V = 64
MLA_Q_RANK = 256
MLA_KV_RANK = 128
MLA_SCALE = (MLA_NOPE + MLA_ROPE) ** -0.5
LOG2_E = 1.4426950408889634
Q_OFF = 3 * D_CONV
KV_OFF = Q_OFF + MLA_Q_RANK
KR_OFF = KV_OFF + MLA_KV_RANK
D_EVEN_PROJ = KR_OFF + LANES
HALO = 16

HGRN_HEADS = 8
HGRN_EXPAND = 128
D_HGRN = HGRN_HEADS * HGRN_EXPAND
HGRN_SCALE = HGRN_EXPAND ** -0.5
HGRN_CHUNK = 64

VMEM_LIMIT = 56 * 1024 * 1024


def _dot(a, b):
    return jnp.dot(a, b, preferred_element_type=F32)


def _dot_nt(a, b):
    return lax.dot_general(a, b, (((1,), (1,)), ((), ())), preferred_element_type=F32)


def _dot_tn(a, b):
    return lax.dot_general(a, b, (((0,), (0,)), ((), ())), preferred_element_type=F32)


def _silu(v):
    return v * jax.nn.sigmoid(v)


def _rms(v):
    return v * lax.rsqrt(jnp.mean(v * v, axis=-1, keepdims=True) + EPS)


def _norm_mod(x, g, shift, scale):
    return _rms(x) * (g * (1.0 + scale)) + shift


def _const_spec(shape):
    nd = len(shape)
    return pl.BlockSpec(shape, lambda *_: (0,) * nd, pipeline_mode=pl.Buffered(1))


def _params(*sem):
    return pltpu.CompilerParams(dimension_semantics=sem, vmem_limit_bytes=VMEM_LIMIT)


def _ada_kernel(cond_ref, w_ref, b_ref, o_ref):
    s = _silu(cond_ref[...]).astype(BF16)
    o_ref[0] = _dot(s, w_ref[0].astype(BF16)) + b_ref[0]


def _ada_mods(cond, ada_w, ada_b):
    depth, d, n = ada_w.shape
    rows = cond.shape[0]
    tn = 1024
    return pl.pallas_call(
        _ada_kernel,
        grid=(depth, n // tn),
        in_specs=[
            pl.BlockSpec((rows, d), lambda l, j: (0, 0)),
            pl.BlockSpec((1, d, tn), lambda l, j: (l, 0, j)),
            pl.BlockSpec((1, 1, tn), lambda l, j: (l, 0, j)),
        ],
        out_specs=pl.BlockSpec((1, rows, tn), lambda l, j: (l, 0, j)),
        out_shape=jax.ShapeDtypeStruct((depth, rows, n), F32),
        compiler_params=_params("arbitrary", "arbitrary"),
        name="ada_mods",
    )(cond, ada_w, ada_b.reshape(depth, 1, n))


def _ffn_kernel(*refs, j, tf, mix, final):
    refs = list(refs)
    x_ref = refs.pop(0)
    if mix:
        a_ref, b_ref, wo_ref = refs.pop(0), refs.pop(0), refs.pop(0)
    mods_ref, g_ref, w1_ref, w3_ref, w2_ref = refs[:5]
    refs = refs[5:]
    if final:
        gf_ref = refs.pop(0)
    o_ref = refs.pop(0)

    x = x_ref[0]
    ms = mods_ref[0]
    if mix:
        dc = a_ref.shape[-1]
        y = _dot(a_ref[0], wo_ref[:dc, :]) + _dot(b_ref[0], wo_ref[dc:, :])
        x = x + ms[5:6] * y
    u = _norm_mod(x, g_ref[...], ms[3 * j:3 * j + 1], ms[3 * j + 1:3 * j + 2]).astype(BF16)
    acc = jnp.zeros(x.shape, F32)
    for c in range(w1_ref.shape[1] // tf):
        sl = slice(c * tf, (c + 1) * tf)
        h = (_silu(_dot(u, w1_ref[:, sl])) * _dot(u, w3_ref[:, sl])).astype(BF16)
        acc = acc + _dot(h, w2_ref[sl, :])
    out = x + (0.5 * ms[3 * j + 2:3 * j + 3]) * acc
    if final:
        out = _rms(out) * gf_ref[...]
    o_ref[0] = out


def _ffn_half(x, mods, g, w1, w3, w2, widx, *, j, mix=None, final_g=None, tm=1024):
    bsz, n, d = x.shape
    tm = min(tm, n)

    def picked(w):
        return pl.BlockSpec((None, None) + w.shape[2:], lambda *_: widx + (0, 0), pipeline_mode=pl.Buffered(1))

    per_batch = mods.shape[0] > 1
    mods_map = (lambda b, t: (b, 0, 0)) if per_batch else (lambda b, t: (0, 0, 0))
    row = lambda b, t: (b, t, 0)
    args = [x]
    specs = [pl.BlockSpec((1, tm, d), row)]
    if mix is not None:
        a, bb, wo = mix
        args += [a, bb, wo]
        specs += [pl.BlockSpec((1, tm, a.shape[-1]), row), pl.BlockSpec((1, tm, bb.shape[-1]), row),
                  _const_spec(wo.shape)]
    args += [mods, g.reshape(1, d), w1, w3, w2]
    specs += [pl.BlockSpec((1, N_MOD, d), mods_map), _const_spec((1, d)), picked(w1), picked(w3), picked(w2)]
    if final_g is not None:
        args.append(final_g.reshape(1, d))
        specs.append(_const_spec((1, d)))
    kern = functools.partial(_ffn_kernel, j=j, tf=256, mix=mix is not None, final=final_g is not None)
    return pl.pallas_call(
        kern,
        grid=(bsz, n // tm),
        in_specs=specs,
        out_specs=pl.BlockSpec((1, tm, d), row),
        out_shape=jax.ShapeDtypeStruct((bsz, n, d), F32),
        compiler_params=_params("parallel", "parallel"),
        name="ffn_half",
    )(*args)


def _even_proj_kernel(x_ref, xp_ref, xn_ref, mods_ref, g_ref, win_ref, cw_ref, gq_ref, wq_ref, gkv_ref, wkv_ref,
                      cos_ref, sin_ref, a_ref, q_ref, k_ref, v_ref, *, n_tok):
    tm = x_ref.shape[1]
    t = pl.program_id(1)
    ms = mods_ref[0]
    xe = jnp.concatenate([xp_ref[0], x_ref[0], xn_ref[0]], axis=0)
    u = _norm_mod(xe, g_ref[...], ms[3:4], ms[4:5]).astype(BF16)
    p = _dot(u, win_ref[...])

    gcv = p[:, D_CONV:2 * D_CONV] * p[:, 2 * D_CONV:3 * D_CONV]
    rows = lax.broadcasted_iota(jnp.int32, (tm + 2 * HALO, 1), 0) + (t * tm - HALO)
    gcv = jnp.where((rows >= 0) & (rows < n_tok), gcv, 0.0)
    prev = pltpu.roll(gcv, 1, 0)[HALO:HALO + tm]
    nxt = pltpu.roll(gcv, tm + 2 * HALO - 1, 0)[HALO:HALO + tm]
    cw = cw_ref[...]
    conv = prev * cw[0:1] + gcv[HALO:HALO + tm] * cw[1:2] + nxt * cw[2:3]
    pm = p[HALO:HALO + tm]
    a_ref[0] = (pm[:, :D_CONV] * conv).astype(BF16)

    cos = cos_ref[...]
    sin = sin_ref[...]
    qa = _dot((_rms(pm[:, Q_OFF:KV_OFF]) * (gq_ref[...] * (MLA_SCALE * LOG2_E))).astype(BF16), wq_ref[...])
    kv = _dot((_rms(pm[:, KV_OFF:KR_OFF]) * gkv_ref[...]).astype(BF16), wkv_ref[...])
    krb = pm[:, KR_OFF:]
    kr = krb * cos + pltpu.roll(krb, LANES - MLA_ROPE, 1) * sin
    lane = lax.broadcasted_iota(jnp.int32, (tm, LANES), 1)
    low = lane < MLA_NOPE
    for h in range(MLA_HEADS):
        sl = slice(h * LANES, (h + 1) * LANES)
        qh = qa[:, sl]
        q_ref[0, h] = (qh * cos + pltpu.roll(qh, LANES - MLA_ROPE, 1) * sin).astype(BF16)
        kvh = kv[:, sl]
        k_ref[0, h] = jnp.where(low, kvh, kr).astype(BF16)
        if h % 2 == 0:
            vh = jnp.where(low, pltpu.roll(kvh, MLA_NOPE, 1), jnp.where(lane == MLA_V, 1.0, 0.0))
        else:
            vh = jnp.where(low, jnp.where(lane == 0, 1.0, 0.0), kvh)
        v_ref[0, h] = vh.astype(BF16)


def _even_proj(x, mods, g, w_in, conv_w, gq, wq, gkv, wkv, cos, sin, *, tm=1024):
    bsz, n, d = x.shape
    tm = min(tm, n)
    nh = tm // HALO
    last = n // HALO - 1
    per_batch = mods.shape[0] > 1
    mods_map = (lambda b, t: (b, 0, 0)) if per_batch else (lambda b, t: (0, 0, 0))
    row = lambda b, t: (b, t, 0)
    head_spec = pl.BlockSpec((1, MLA_HEADS, tm, LANES), lambda b, t: (b, 0, t, 0))
    head_shape = jax.ShapeDtypeStruct((bsz, MLA_HEADS, n, LANES), BF16)
    return pl.pallas_call(
        functools.partial(_even_proj_kernel, n_tok=n),
        grid=(bsz, n // tm),
        in_specs=[
            pl.BlockSpec((1, tm, d), row),
            pl.BlockSpec((1, HALO, d), lambda b, t: (b, jnp.maximum(t * nh - 1, 0), 0)),
            pl.BlockSpec((1, HALO, d), lambda b, t: (b, jnp.minimum((t + 1) * nh, last), 0)),
            pl.BlockSpec((1, N_MOD, d), mods_map),
            _const_spec((1, d)),
            _const_spec(w_in.shape),
            _const_spec(conv_w.shape),
            _const_spec((1, MLA_Q_RANK)),
            _const_spec(wq.shape),
            _const_spec((1, MLA_KV_RANK)),
            _const_spec(wkv.shape),
            pl.BlockSpec((tm, LANES), lambda b, t: (t, 0)),
            pl.BlockSpec((tm, LANES), lambda b, t: (t, 0)),
        ],
        out_specs=[pl.BlockSpec((1, tm, D_CONV), row), head_spec, head_spec, head_spec],
        out_shape=[jax.ShapeDtypeStruct((bsz, n, D_CONV), BF16), head_shape, head_shape, head_shape],
        compiler_params=_params("parallel", "parallel"),
        name="even_proj",
    )(x, x, x, mods, g.reshape(1, d), w_in, conv_w, gq.reshape(1, -1), wq, gkv.reshape(1, -1), wkv, cos, sin)


def _attn_kernel(*refs, seg_chunks):
    q_ref = refs[0]
    nseg = len(seg_chunks)
    kv_refs = refs[1:1 + 2 * nseg]
    o_ref = refs[1 + 2 * nseg]
    m_ref, acc_ref, s_ref = refs[2 + 2 * nseg:]
    tq = q_ref.shape[2]

    m_ref[...] = jnp.full(m_ref.shape, -jnp.inf, F32)
    acc_ref[...] = jnp.zeros(acc_ref.shape, F32)

    for s, (tk, nk) in enumerate(seg_chunks):
        k_ref, v_ref = kv_refs[2 * s], kv_refs[2 * s + 1]

        def scores(hh, off, k_ref=k_ref, tk=tk):
            s_ref[hh, :, :tk] = _dot_nt(q_ref[0, hh], k_ref[0, hh, pl.ds(off, tk), :])

        def accumulate(hh, off, v_ref=v_ref, tk=tk):
            sc = s_ref[hh, :, :tk]
            m_prev = m_ref[hh]
            m_new = jnp.maximum(m_prev, jnp.max(sc, axis=1, keepdims=True))
            p = jnp.exp2(sc - jnp.concatenate([m_new] * (tk // LANES), axis=1))
            acc_ref[hh] = (acc_ref[hh] * jnp.exp2(m_prev - m_new)
                           + _dot(p.astype(BF16), v_ref[0, hh, pl.ds(off, tk), :]))
            m_ref[hh] = m_new

        def step(c, carry, tk=tk, scores=scores, accumulate=accumulate, prefetch=True):
            off = c * tk if isinstance(c, int) else pl.multiple_of(c * tk, tk)
            scores(1, off)
            accumulate(0, off)
            if prefetch:
                scores(0, pl.multiple_of((c + 1) * tk, tk))
            accumulate(1, off)
            return carry

        scores(0, 0)
        if nk > 1:
            lax.fori_loop(0, nk - 1, step, 0, unroll=True)
        step(nk - 1, 0, prefetch=False)

    lane = lax.broadcasted_iota(jnp.int32, (tq, LANES), 1)
    acc_e, acc_o = acc_ref[0], acc_ref[1]
    den_e = jnp.sum(jnp.where(lane == MLA_V, acc_e, 0.0), axis=1, keepdims=True)
    den_o = jnp.sum(jnp.where(lane == 0, acc_o, 0.0), axis=1, keepdims=True)
    o_ref[0] = jnp.where(lane < MLA_V, acc_e / den_e, acc_o / den_o).astype(o_ref.dtype)


def _attention(q, segs, *, tq=512, tk=2048):
    bsz, nh, n, _ = q.shape
    tq = min(tq, n)
    args = [q]
    specs = [pl.BlockSpec((1, 2, tq, LANES), lambda b, h, t: (b, h, t, 0))]
    seg_chunks = []
    for k, v in segs:
        ns = k.shape[2]
        tks = min(tk, ns)
        seg_chunks.append((tks, ns // tks))
        spec = pl.BlockSpec((1, 2, ns, LANES), lambda b, h, t: (b, h, 0, 0))
        args += [k, v]
        specs += [spec, spec]
    return pl.pallas_call(
        functools.partial(_attn_kernel, seg_chunks=tuple(seg_chunks)),
        grid=(bsz, nh // 2, n // tq),
        in_specs=specs,
        out_specs=pl.BlockSpec((1, tq, LANES), lambda b, h, t: (b, t, h)),
        out_shape=jax.ShapeDtypeStruct((bsz, n, nh // 2 * LANES), BF16),
        scratch_shapes=[pltpu.VMEM((2, tq, LANES), F32), pltpu.VMEM((2, tq, LANES), F32),
                        pltpu.VMEM((2, tq, max(t for t, _ in seg_chunks)), F32)],
        compiler_params=_params("parallel", "parallel", "parallel"),
        name="mla_attention",
    )(*args)


def _split3(v):
    hi = v.astype(BF16)
    r = v - hi.astype(F32)
    mid = r.astype(BF16)
    lo = (r - mid.astype(F32)).astype(BF16)
    return hi, mid, lo


def _gla_rows(st_ref, lb, z, val, q, reverse, dst, group=256):
    n = z.shape[0]
    starts = list(range(0, n, group))
    starts = list(reversed(starts)) if reverse else starts
    preps = [_gla_prep(lb, z[r0:r0 + group], val[r0:r0 + group], None if q is None else q[r0:r0 + group], reverse)
             for r0 in starts]
    for r0, prep in zip(starts, preps):
        _gla_scan(st_ref, prep, reverse, None if dst is None else dst.at[r0:r0 + group])


def _gla_prep(lb, z, val, q, reverse):
    n = z.shape[0]
    nch = n // HGRN_CHUNK
    r = lax.broadcasted_iota(jnp.int32, (n, n), 0)
    c = lax.broadcasted_iota(jnp.int32, (n, n), 1)
    same = (r // HGRN_CHUNK) == (c // HGRN_CHUNK)
    tri = jnp.where(same & ((c >= r) if reverse else (c <= r)), 1.0, 0.0).astype(BF16)

    f = lb + (1.0 - lb) * jax.nn.sigmoid(z)
    kk = 1.0 - f
    hi, mid, lo = _split3(jnp.log(f))
    b = _dot(tri, hi) + _dot(tri, mid) + _dot(tri, lo)
    tot_row = [ci * HGRN_CHUNK + (0 if reverse else HGRN_CHUNK - 1) for ci in range(nch)]
    btot = [b[tr:tr + 1] for tr in tot_row]
    rest = jnp.concatenate([jnp.broadcast_to(bt, (HGRN_CHUNK, bt.shape[1])) for bt in btot], axis=0) - b
    k_end = (kk * jnp.exp(rest)).astype(BF16)
    vb = val.astype(BF16)
    if q is None:
        return None, None, k_end, vb, btot
    return (q * jnp.exp(b)).astype(BF16), (kk * jnp.exp(-b)).astype(BF16), k_end, vb, btot


def _gla_scan(st_ref, prep, reverse, dst):
    q_t, k_t, k_end, vb, btot = prep
    nch = len(btot)
    order = list(reversed(range(nch))) if reverse else list(range(nch))
    if q_t is not None:
        rr = lax.broadcasted_iota(jnp.int32, (HGRN_CHUNK, HGRN_CHUNK), 0)
        cc = lax.broadcasted_iota(jnp.int32, (HGRN_CHUNK, HGRN_CHUNK), 1)
        mask = (cc >= rr) if reverse else (cc <= rr)

    def blk(a, ci, h):
        return a[ci * HGRN_CHUNK:(ci + 1) * HGRN_CHUNK, h * HGRN_EXPAND:(h + 1) * HGRN_EXPAND]

    intra, dst_upd = {}, {}
    for ci in order:
        for h in range(HGRN_HEADS):
            if q_t is not None:
                att = jnp.where(mask, _dot_nt(blk(q_t, ci, h), blk(k_t, ci, h)), 0.0).astype(BF16)
                intra[ci, h] = _dot(att, blk(vb, ci, h))
            dst_upd[ci, h] = _dot_tn(blk(vb, ci, h), blk(k_end, ci, h))
    for ci in order:
        decay = jnp.exp(btot[ci])
        for h in range(HGRN_HEADS):
            st = st_ref[h]
            if q_t is not None:
                dst[ci * HGRN_CHUNK:(ci + 1) * HGRN_CHUNK, h * HGRN_EXPAND:(h + 1) * HGRN_EXPAND] = (
                    intra[ci, h] + _dot_nt(blk(q_t, ci, h), st.astype(BF16)))
            st_ref[h] = st * decay[:, h * HGRN_EXPAND:(h + 1) * HGRN_EXPAND] + dst_upd[ci, h]


def _hgrn_bwd_kernel(x_ref, xc_ref, mods_ref, modc_ref, g_ref, win_ref, lb_ref,
                     ob_ref, q_ref, i_ref, zf_ref, gate_ref, st_ref):
    t = pl.program_id(1)
    lb = lb_ref[...]

    @pl.when(t == 0)
    def _():
        st_ref[...] = jnp.zeros(st_ref.shape, F32)
        mc = modc_ref[0]
        u = _norm_mod(xc_ref[0], g_ref[...], mc[3:4], mc[4:5]).astype(BF16)
        _gla_rows(st_ref, lb, _dot(u, win_ref[:, 3 * D_HGRN:4 * D_HGRN]), _dot(u, win_ref[:, D_HGRN:2 * D_HGRN]),
                  None, True, None)

    @pl.when(t > 0)
    def _():
        ms = mods_ref[0]
        u = _norm_mod(x_ref[0], g_ref[...], ms[3:4], ms[4:5]).astype(BF16)

        p = _dot(u, win_ref[...])
        q_ref[0] = p[:, :D_HGRN]
        i_ref[0] = p[:, D_HGRN:2 * D_HGRN]
        zf_ref[0] = p[:, 2 * D_HGRN:3 * D_HGRN]
        gate_ref[0] = p[:, 4 * D_HGRN:]
        _gla_rows(st_ref, lb, p[:, 3 * D_HGRN:4 * D_HGRN], p[:, D_HGRN:2 * D_HGRN], p[:, :D_HGRN] * HGRN_SCALE,
                  True, ob_ref.at[0])


def _hgrn_fwd_kernel(x_ref, xc_ref, mods_ref, modc_ref, g_ref, wc_ref, lb_ref, q_ref, i_ref, zf_ref, gate_ref,
                     ob_ref, gn_ref, wo_ref, o_ref, st_ref, of_ref):
    t = pl.program_id(1)
    lb = lb_ref[...]

    @pl.when(t == 0)
    def _():
        st_ref[...] = jnp.zeros(st_ref.shape, F32)
        mc = modc_ref[0]
        u = _norm_mod(xc_ref[0], g_ref[...], mc[3:4], mc[4:5]).astype(BF16)
        p = _dot(u, wc_ref[...])
        _gla_rows(st_ref, lb, p[:, D_HGRN:], p[:, :D_HGRN], None, False, None)

    @pl.when(t > 0)
    def _():
        _gla_rows(st_ref, lb, zf_ref[0], i_ref[0], q_ref[0] * HGRN_SCALE, False, of_ref)
        o = of_ref[...] + ob_ref[0]
        gn = gn_ref[...]
        ys = [_rms(o[:, h * HGRN_EXPAND:(h + 1) * HGRN_EXPAND]) * gn for h in range(HGRN_HEADS)]
        y = (jnp.concatenate(ys, axis=1) * _silu(gate_ref[0])).astype(BF16)
        o_ref[0] = x_ref[0] + mods_ref[0][5:6] * _dot(y, wo_ref[...])


def _hgrn_mixer(x, xc, mods_x, mods_c, g, w_in, lb, gn, wo, *, tm=512):
    bsz, n, d = x.shape
    nc = xc.shape[1]
    tm = min(tm, n)
    nt = n // tm
    bmap = lambda b, t: (b, nt - 1 - jnp.maximum(t - 1, 0), 0)
    fmap = lambda b, t: (b, jnp.maximum(t - 1, 0), 0)
    common = [
        pl.BlockSpec((1, nc, d), lambda b, t: (b, 0, 0)),
        pl.BlockSpec((1, N_MOD, d), lambda b, t: (b, 0, 0)),
        pl.BlockSpec((1, N_MOD, d), lambda b, t: (0, 0, 0)),
        _const_spec((1, d)),
    ]
    state = pltpu.VMEM((HGRN_HEADS, HGRN_EXPAND, HGRN_EXPAND), F32)
    tile_shape = jax.ShapeDtypeStruct((bsz, n, D_HGRN), F32)
    o_b, q, i, zf, gate = pl.pallas_call(
        _hgrn_bwd_kernel,
        grid=(bsz, nt + 1),
        in_specs=[pl.BlockSpec((1, tm, d), bmap)] + common + [_const_spec(w_in.shape), _const_spec((1, D_HGRN))],
        out_specs=[pl.BlockSpec((1, tm, D_HGRN), bmap)] * 5,
        out_shape=[tile_shape] * 5,
        scratch_shapes=[state],
        compiler_params=_params("parallel", "arbitrary"),
        name="hgrn_bwd",
    )(x, xc, mods_x, mods_c, g.reshape(1, d), w_in, lb[1].reshape(1, D_HGRN))
    ftile = pl.BlockSpec((1, tm, D_HGRN), fmap)
    w_ctx = w_in[:, D_HGRN:3 * D_HGRN]
    return pl.pallas_call(
        _hgrn_fwd_kernel,
        grid=(bsz, nt + 1),
        in_specs=[pl.BlockSpec((1, tm, d), fmap)] + common + [_const_spec(w_ctx.shape), _const_spec((1, D_HGRN))]
        + [ftile] * 5 + [_const_spec((1, HGRN_EXPAND)), _const_spec(wo.shape)],
        out_specs=pl.BlockSpec((1, tm, d), fmap),
        out_shape=jax.ShapeDtypeStruct((bsz, n, d), F32),
        scratch_shapes=[state, pltpu.VMEM((tm, D_HGRN), F32)],
        compiler_params=_params("parallel", "arbitrary"),
        name="hgrn_fwd",
    )(x, xc, mods_x, mods_c, g.reshape(1, d), w_ctx, lb[0].reshape(1, D_HGRN), q, i, zf, gate, o_b,
      gn.reshape(1, HGRN_EXPAND), wo)


def _rope_tables(n):
    rows = n // GRID_W
    n_freq = MLA_ROPE // 4
    inv = ROPE_BASE ** (-jnp.arange(n_freq, dtype=F32) / n_freq)
    ar = jnp.arange(rows, dtype=jnp.int32).astype(F32)[:, None] * inv
    ac = jnp.arange(GRID_W, dtype=jnp.int32).astype(F32)[:, None] * inv

    def per_token(tab_r, tab_c):
        tr = jnp.broadcast_to(tab_r[:, None, :], (rows, GRID_W, n_freq)).reshape(n, n_freq)
        tc = jnp.broadcast_to(tab_c[None, :, :], (rows, GRID_W, n_freq)).reshape(n, n_freq)
        return tr, tc

    cr, cc = per_token(jnp.cos(ar), jnp.cos(ac))
    sr, sc = per_token(jnp.sin(ar), jnp.sin(ac))
    cos = jnp.concatenate([cr, cr, cc, cc], axis=1)
    sin = jnp.concatenate([-sr, sr, -sc, sc], axis=1)
    ones = jnp.ones((n, MLA_NOPE), F32)
    zeros = jnp.zeros((n, MLA_ROPE), F32)
    return (jnp.concatenate([ones, cos, zeros], axis=1),
            jnp.concatenate([jnp.zeros((n, MLA_NOPE), F32), sin, zeros], axis=1))


def _swap_rope_cols(w):
    q = MLA_ROPE // 4
    return jnp.concatenate([w[:, q:2 * q], w[:, :q], w[:, 3 * q:], w[:, 2 * q:3 * q]], axis=1)


def _even_weights(w_in, w_uq):
    kr = w_in[:, KR_OFF:]
    w_in = jnp.concatenate([w_in[:, :KR_OFF], jnp.zeros((w_in.shape[0], MLA_NOPE), w_in.dtype), kr,
                            _swap_rope_cols(kr)], axis=1)
    wq = w_uq.reshape(MLA_Q_RANK, MLA_HEADS, MLA_NOPE + MLA_ROPE)
    blocks = []
    for h in range(MLA_HEADS):
        rope = wq[:, h, MLA_NOPE:]
        blocks += [wq[:, h, :MLA_NOPE], rope, _swap_rope_cols(rope)]
    return w_in.astype(BF16), jnp.concatenate(blocks, axis=1).astype(BF16)


def kernel(x, c, ctx, c_ctx, ada_w, ada_b, norm_g, ffn_w1, ffn_w3, ffn_w2, even_w_in, even_conv_w, mla_q_norm_g,
           mla_w_uq, mla_kv_norm_g, mla_w_ukv, even_w_out, odd_w_in, hgrn_lb_logits, hgrn_g_norm_g, odd_w_out,
           final_norm_g):
    bsz, n, d = x.shape
    depth = ada_w.shape[0]
    n_ctx = ctx.shape[1]

    cond = jnp.concatenate([c, c_ctx[None], jnp.zeros((8 - bsz - 1, d), F32)], axis=0)
    mods = _ada_mods(cond, ada_w, ada_b).reshape(depth, 8, N_MOD, d)

    lb_p = jax.nn.softmax(hgrn_lb_logits.astype(F32), axis=0)
    lb_table = jnp.cumsum(lb_p, axis=0) - lb_p[0]

    w1, w3, w2 = ffn_w1.astype(BF16), ffn_w3.astype(BF16), ffn_w2.astype(BF16)
    cos_x, sin_x = _rope_tables(n)
    cos_c = jnp.concatenate([jnp.ones((n_ctx, MLA_NOPE + MLA_ROPE), F32), jnp.zeros((n_ctx, MLA_ROPE), F32)], axis=1)
    sin_c = jnp.zeros((n_ctx, LANES), F32)

    def flat(a):
        return a.reshape(1, bsz * n_ctx, a.shape[-1])

    h = ctx
    for l in range(depth):
        need_ctx_out = l < depth - 1
        mx, mc = mods[l, :bsz], mods[l, bsz:bsz + 1]
        last = l == depth - 1
        x = _ffn_half(x, mx, norm_g[l, 0], w1, w3, w2, (l, 0), j=0)
        h = _ffn_half(flat(h), mc, norm_g[l, 0], w1, w3, w2, (l, 0), j=0).reshape(ctx.shape)
        if l % 2 == 0:
            e = l // 2
            w_in, wq = _even_weights(even_w_in[e], mla_w_uq[e])
            wkv, wo = mla_w_ukv[e].astype(BF16), even_w_out[e].astype(BF16)
            proj = functools.partial(_even_proj, g=norm_g[l, 1], w_in=w_in, conv_w=even_conv_w[e],
                                     gq=mla_q_norm_g[e], wq=wq, gkv=mla_kv_norm_g[e], wkv=wkv)
            a_x, q_x, k_x, v_x = proj(x, mx, cos=cos_x, sin=sin_x)
            a_c, q_c, k_c, v_c = proj(h, mc, cos=cos_c, sin=sin_c)
            b_x = _attention(q_x, [(k_c, v_c), (k_x, v_x)])
            x = _ffn_half(x, mx, norm_g[l, 2], w1, w3, w2, (l, 1), j=2, mix=(a_x, b_x, wo),
                          final_g=final_norm_g if last else None)
            if need_ctx_out:
                b_c = _attention(q_c, [(k_c, v_c)])
                h = _ffn_half(flat(h), mc, norm_g[l, 2], w1, w3, w2, (l, 1), j=2,
                              mix=(flat(a_c), flat(b_c), wo)).reshape(ctx.shape)
        else:
            if need_ctx_out:
                raise NotImplementedError("context output of an HGRN2 layer is only needed for depth > 2")
            o = l // 2
            x = _hgrn_mixer(x, h, mx, mc, norm_g[l, 1], odd_w_in[o].astype(BF16), lb_table[l], hgrn_g_norm_g[o],
                            odd_w_out[o].astype(BF16))
            x = _ffn_half(x, mx, norm_g[l, 2], w1, w3, w2, (l, 1), j=2,
                          final_g=final_norm_g if last else None)
    return x
```
